```python
import math
import jax
import jax.numpy as jnp
from jax import lax
import numpy as np

D_MODEL = 1024
BATCH = 1
SEQ = 16384
DEPTH = 2

GRID_W = 64
CTX_LEN = 256
SSM_WIDTH = 512
SSM_GROUP = 16
SSM_GROUPS = SSM_WIDTH // SSM_GROUP
SSM_STATE = 64
CONV_WIDTH = 512
CONV_SIZE = 31
IN_COLS = SSM_WIDTH + 2 * CONV_WIDTH + 2 * D_MODEL
N_EXPERTS = 16
N_EXPERT_GROUPS = 4
EXPERTS_PER_GROUP = N_EXPERTS // N_EXPERT_GROUPS
TOP_K = 2
D_EXPERT = 512
N_MOD = 6
NORM_EPS = 1e-6

kernel_name = "hybrid_s5_conformer_moe_diffusion_block"

F32 = jnp.float32


def rms_norm(x, g):
    xf = x.astype(F32)
    y = xf * lax.rsqrt(jnp.mean(xf * xf, axis=-1, keepdims=True) + NORM_EPS)
    return (y * g.astype(F32)).astype(x.dtype)


def layer_norm(x, g, b):
    xf = x.astype(F32)
    mu = jnp.mean(xf, axis=-1, keepdims=True)
    xc = xf - mu
    y = xc * lax.rsqrt(jnp.mean(xc * xc, axis=-1, keepdims=True) + NORM_EPS)
    return (y * g.astype(F32) + b.astype(F32)).astype(x.dtype)


def modulate(h, shift, scale):
    return h * (1.0 + scale) + shift


def ssm_discretise(a_re, a_im, log_dt, b_re, b_im):
    lam_re = jnp.minimum(a_re.astype(F32), -1e-4)
    lam_im = a_im.astype(F32)
    dt = jnp.exp(log_dt.astype(F32))[:, None]
    mag = jnp.exp(lam_re * dt)
    abar_re = mag * jnp.cos(lam_im * dt)
    abar_im = mag * jnp.sin(lam_im * dt)
    n_re, n_im = abar_re - 1.0, abar_im
    den = lam_re * lam_re + lam_im * lam_im
    coef_re = (n_re * lam_re + n_im * lam_im) / den
    coef_im = (n_im * lam_re - n_re * lam_im) / den
    br, bi = b_re.astype(F32), b_im.astype(F32)
    bbar_re = coef_re[..., None] * br - coef_im[..., None] * bi
    bbar_im = coef_re[..., None] * bi + coef_im[..., None] * br
    return abar_re, abar_im, bbar_re, bbar_im


def ssm_scan(u, abar_re, abar_im, bbar_re, bbar_im, h0):
    bu_re = jnp.einsum('blgc,gpc->blgp', u, bbar_re)
    bu_im = jnp.einsum('blgc,gpc->blgp', u, bbar_im)
    a_re = jnp.broadcast_to(abar_re, bu_re.shape)
    a_im = jnp.broadcast_to(abar_im, bu_re.shape)

    def combine(early, late):
        ar1, ai1, br1, bi1 = early
        ar2, ai2, br2, bi2 = late
        return (ar2 * ar1 - ai2 * ai1,
                ar2 * ai1 + ai2 * ar1,
                ar2 * br1 - ai2 * bi1 + br2,
                ar2 * bi1 + ai2 * br1 + bi2)

    pr, pi, hr, hi = lax.associative_scan(combine, (a_re, a_im, bu_re, bu_im), axis=1)
    if h0 is not None:
        h0r, h0i = h0[0][:, None], h0[1][:, None]
        hr = hr + pr * h0r - pi * h0i
        hi = hi + pr * h0i + pi * h0r
    return hr, hi


def s5_bidirectional(u, disc, c_re, c_im, init_states, with_output):
    bsz, length, _ = u.shape
    ug = u.astype(F32).reshape(bsz, length, SSM_GROUPS, SSM_GROUP)
    y = None
    finals = []
    for k in range(2):
        seq = ug if k == 0 else jnp.flip(ug, axis=1)
        h0 = None if init_states is None else init_states[k]
        hr, hi = ssm_scan(seq, *disc[k], h0)
        finals.append((hr[:, -1], hi[:, -1]))
        if with_output:
            yk = (jnp.einsum('blgp,gcp->blgc', hr, c_re[k].astype(F32))
                  - jnp.einsum('blgp,gcp->blgc', hi, c_im[k].astype(F32)))
            yk = yk if k == 0 else jnp.flip(yk, axis=1)
            y = yk if y is None else y + yk
    if with_output:
        y = y.reshape(bsz, length, SSM_WIDTH).astype(u.dtype)
    return y, finals


def depthwise_conv(z, w, b):
    ch = z.shape[-1]
    out = lax.conv_general_dilated(
        z, w[:, None, :].astype(z.dtype), window_strides=(1,),
        padding=[(CONV_SIZE // 2, CONV_SIZE // 2)],
        dimension_numbers=('NWC', 'WIO', 'NWC'), feature_group_count=ch)
    return out + b


def grid_depthwise_conv(z, w, b, along_rows):
    bsz, length, ch = z.shape
    rows = length // GRID_W
    g = z.reshape(bsz, rows, GRID_W, ch)
    if along_rows:
        out = depthwise_conv(g.reshape(bsz * rows, GRID_W, ch), w, b)
        return out.reshape(bsz, length, ch)
    gt = jnp.swapaxes(g, 1, 2).reshape(bsz * GRID_W, rows, ch)
    out = depthwise_conv(gt, w, b).reshape(bsz, GRID_W, rows, ch)
    return jnp.swapaxes(out, 1, 2).reshape(bsz, length, ch)


def mixer_merge(proj, y_ssm, conv_fn, d_skip, w_a_val, w_a_gate, conv_ln_g, conv_ln_b, w_b_out, w_out):
    u = proj[..., :SSM_WIDTH]
    v = proj[..., SSM_WIDTH:SSM_WIDTH + 2 * CONV_WIDTH]
    g_a, g_b = jnp.split(jax.nn.sigmoid(proj[..., SSM_WIDTH + 2 * CONV_WIDTH:]), 2, axis=-1)
    ya = jax.nn.gelu(y_ssm + d_skip * u)
    a_out = (ya @ w_a_val) * jax.nn.sigmoid(ya @ w_a_gate)
    v1, v2 = jnp.split(v, 2, axis=-1)
    z = conv_fn(v1 * jax.nn.sigmoid(v2))
    b_out = jax.nn.silu(layer_norm(z, conv_ln_g, conv_ln_b)) @ w_b_out
    return (g_a * a_out + g_b * b_out) @ w_out


def grouped_moe(h, router_w, router_b, w_gate, w_up, w_down):
    lead = h.shape[:-1]
    t = h.reshape(-1, D_MODEL)
    logits = (t @ router_w).astype(F32) + router_b.astype(F32)
    probs = jax.nn.softmax(logits, axis=-1)
    grouped = probs.reshape(-1, N_EXPERT_GROUPS, EXPERTS_PER_GROUP)
    group_score = lax.top_k(grouped, TOP_K)[0].sum(-1)
    best = jnp.argmax(group_score, axis=-1)
    in_group = (jnp.arange(N_EXPERTS) // EXPERTS_PER_GROUP)[None, :] == best[:, None]
    top_p, top_i = lax.top_k(jnp.where(in_group, probs, -1.0), TOP_K)
    top_w = top_p / jnp.sum(top_p, axis=-1, keepdims=True)
    dense_w = jnp.sum(jax.nn.one_hot(top_i, N_EXPERTS, dtype=F32) * top_w[..., None], axis=1)
    dense_w = dense_w.astype(t.dtype)
    out = jnp.zeros_like(t)
    for e in range(N_EXPERTS):
        hid = jax.nn.silu(t @ w_gate[e]) * (t @ w_up[e])
        out = out + dense_w[:, e:e + 1] * (hid @ w_down[e])
    return out.reshape(*lead, D_MODEL)


def setup_inputs(seed: int = 0) -> dict:
    key = jax.random.key(seed)
    ks = iter(jax.random.split(key, 40))
    nrm = lambda shape, s: jax.random.normal(next(ks), shape, F32) * s
    L, G, P, Cg = DEPTH, SSM_GROUPS, SSM_STATE, SSM_GROUP
    a_im_base = jnp.broadcast_to(math.pi * jnp.arange(P, dtype=F32), (L, 2, G, P))
    return {
        "x": nrm((BATCH, SEQ, D_MODEL), 1.0),
        "c": nrm((BATCH, D_MODEL), 1.0),
        "ctx": nrm((BATCH, CTX_LEN, D_MODEL), 1.0),
        "c_ctx": nrm((D_MODEL,), 1.0),
        "ada_w": nrm((L, D_MODEL, N_MOD * D_MODEL), 0.02),
        "ada_b": nrm((L, N_MOD * D_MODEL), 0.02),
        "norm_mix_g": 1.0 + nrm((L, D_MODEL), 0.02),
        "norm_ffn_g": 1.0 + nrm((L, D_MODEL), 0.02),
        "w_in": nrm((L, D_MODEL, IN_COLS), D_MODEL ** -0.5),
        "ssm_a_re": -0.5 + nrm((L, 2, G, P), 0.01),
        "ssm_a_im": a_im_base + nrm((L, 2, G, P), 0.01),
        "ssm_log_dt": jax.random.uniform(next(ks), (L, 2, G), F32, math.log(1e-3), math.log(1e-1)),
        "ssm_b_re": nrm((L, 2, G, P, Cg), (2 * Cg) ** -0.5),
        "ssm_b_im": nrm((L, 2, G, P, Cg), (2 * Cg) ** -0.5),
        "ssm_c_re": nrm((L, 2, G, Cg, P), (2 * P) ** -0.5),
        "ssm_c_im": nrm((L, 2, G, Cg, P), (2 * P) ** -0.5),
        "ssm_d": nrm((L, SSM_WIDTH), 1.0),
        "w_a_val": nrm((L, SSM_WIDTH, D_MODEL), SSM_WIDTH ** -0.5),
        "w_a_gate": nrm((L, SSM_WIDTH, D_MODEL), SSM_WIDTH ** -0.5),
        "conv_w": nrm((L, CONV_SIZE, CONV_WIDTH), CONV_SIZE ** -0.5),
        "conv_b": nrm((L, CONV_WIDTH), 0.02),
        "conv_ln_g": 1.0 + nrm((L, CONV_WIDTH), 0.02),
        "conv_ln_b": nrm((L, CONV_WIDTH), 0.02),
        "w_b_out": nrm((L, CONV_WIDTH, D_MODEL), CONV_WIDTH ** -0.5),
        "w_out": nrm((L, D_MODEL, D_MODEL), D_MODEL ** -0.5),
        "router_w": nrm((D_MODEL, N_EXPERTS), D_MODEL ** -0.5),
        "router_b": nrm((N_EXPERTS,), 0.01),
        "w_exp_gate": nrm((L, N_EXPERTS, D_MODEL, D_EXPERT), D_MODEL ** -0.5),
        "w_exp_up": nrm((L, N_EXPERTS, D_MODEL, D_EXPERT), D_MODEL ** -0.5),
        "w_exp_down": nrm((L, N_EXPERTS, D_EXPERT, D_MODEL), D_EXPERT ** -0.5),
        "final_norm_g": 1.0 + nrm((D_MODEL,), 0.02),
    }


def reference(x, c, ctx, c_ctx, ada_w, ada_b, norm_mix_g, norm_ffn_g, w_in,
              ssm_a_re, ssm_a_im, ssm_log_dt, ssm_b_re, ssm_b_im, ssm_c_re, ssm_c_im, ssm_d,
              w_a_val, w_a_gate, conv_w, conv_b, conv_ln_g, conv_ln_b, w_b_out, w_out,
              router_w, router_b, w_exp_gate, w_exp_up, w_exp_down, final_norm_g):
    cx = ctx
    for l in range(DEPTH):
        last = l == DEPTH - 1
        mod_x = jnp.split((jax.nn.silu(c) @ ada_w[l] + ada_b[l])[:, None, :], N_MOD, axis=-1)
        mod_c = jnp.split(jax.nn.silu(c_ctx) @ ada_w[l] + ada_b[l], N_MOD, axis=-1)

        hx = modulate(rms_norm(x, norm_mix_g[l]), mod_x[0], mod_x[1])
        hc = modulate(rms_norm(cx, norm_mix_g[l]), mod_c[0], mod_c[1])
        px = hx @ w_in[l]
        pc = hc @ (w_in[l][:, :SSM_WIDTH] if last else w_in[l])

        disc = [ssm_discretise(ssm_a_re[l, k], ssm_a_im[l, k], ssm_log_dt[l, k],
                               ssm_b_re[l, k], ssm_b_im[l, k]) for k in range(2)]
        yc, ctx_states = s5_bidirectional(pc[..., :SSM_WIDTH], disc, ssm_c_re[l], ssm_c_im[l],
                                          None, not last)
        yx, _ = s5_bidirectional(px[..., :SSM_WIDTH], disc, ssm_c_re[l], ssm_c_im[l],
                                 ctx_states, True)

        along_rows = (l % 2 == 0)
        cw, cb = conv_w[l], conv_b[l]
        mx = mixer_merge(px, yx, lambda z: grid_depthwise_conv(z, cw, cb, along_rows),
                         ssm_d[l], w_a_val[l], w_a_gate[l], conv_ln_g[l], conv_ln_b[l],
                         w_b_out[l], w_out[l])
        x = x + mod_x[2] * mx

        if not last:
            mc = mixer_merge(pc, yc, lambda z: depthwise_conv(z, cw, cb),
                             ssm_d[l], w_a_val[l], w_a_gate[l], conv_ln_g[l], conv_ln_b[l],
                             w_b_out[l], w_out[l])
            cx = cx + mod_c[2] * mc
            hc2 = modulate(rms_norm(cx, norm_ffn_g[l]), mod_c[3], mod_c[4])
            cx = cx + mod_c[5] * grouped_moe(hc2, router_w, router_b,
                                             w_exp_gate[l], w_exp_up[l], w_exp_down[l])

        hx2 = modulate(rms_norm(x, norm_ffn_g[l]), mod_x[3], mod_x[4])
        x = x + mod_x[5] * grouped_moe(hx2, router_w, router_b,
                                       w_exp_gate[l], w_exp_up[l], w_exp_down[l])
    return rms_norm(x, final_norm_g)
```

```python
import functools
import math

import jax
import jax.numpy as jnp
from jax import lax
from jax.experimental import pallas as pl
from jax.experimental.pallas import tpu as pltpu

F32 = jnp.float32
BF16 = jnp.bfloat16

D_MODEL = 1024
GRID_W = 64
SSM_WIDTH = 512
SSM_GROUP = 16
SSM_GROUPS = SSM_WIDTH // SSM_GROUP
SSM_STATE = 64
CONV_WIDTH = 512
CONV_SIZE = 31
CONV_HALF = CONV_SIZE // 2
IN_COLS = SSM_WIDTH + 2 * CONV_WIDTH + 2 * D_MODEL
N_EXPERTS = 16
N_EXPERT_GROUPS = 4
EXPERTS_PER_GROUP = N_EXPERTS // N_EXPERT_GROUPS
D_EXPERT = 512
N_MOD = 6
NORM_EPS = 1e-6

TM = 256
LANES = 128
SUBLANES = 8
SEG = TM // SUBLANES
KB = SSM_WIDTH // LANES
GPB = LANES // SSM_GROUP
SB = GPB * SSM_STATE
PAD = 16
VMEM_LIMIT = 56 * 1024 * 1024


def _cparams(sem):
    return pltpu.CompilerParams(dimension_semantics=sem, vmem_limit_bytes=VMEM_LIMIT)


def _split_bf16(a):
    hi = a.astype(BF16)
    lo = (a - hi.astype(F32)).astype(BF16)
    return hi, lo


def _mods_kernel(c_ref, w_ref, b_ref, o_ref):
    s = c_ref[...]
    s = s * jax.nn.sigmoid(s)
    s_hi, s_lo = _split_bf16(s)
    w_hi, w_lo = _split_bf16(w_ref[...])
    acc = jnp.dot(s_hi, w_hi, preferred_element_type=F32)
    acc += jnp.dot(s_lo, w_hi, preferred_element_type=F32)
    acc += jnp.dot(s_hi, w_lo, preferred_element_type=F32)
    o_ref[...] = acc + b_ref[...]


def _mods(cc, ada_w, ada_b):
    depth = ada_w.shape[0]
    cols = ada_w.shape[2]
    cb = 1536
    return pl.pallas_call(
        _mods_kernel,
        grid=(depth, cols // cb),
        in_specs=[
            pl.BlockSpec((SUBLANES, D_MODEL), lambda l, j: (0, 0)),
            pl.BlockSpec((None, D_MODEL, cb), lambda l, j: (l, 0, j)),
            pl.BlockSpec((None, 1, cb), lambda l, j: (l, 0, j)),
        ],
        out_specs=pl.BlockSpec((None, SUBLANES, cb), lambda l, j: (l, 0, j)),
        out_shape=jax.ShapeDtypeStruct((depth, SUBLANES, cols), F32),
        compiler_params=_cparams(("arbitrary", "arbitrary")),
        name="mods",
    )(cc, ada_w, ada_b.reshape(depth, 1, cols))


def _norm_mod(x, g, shift, scale):
    y = x * lax.rsqrt(jnp.mean(x * x, axis=-1, keepdims=True) + NORM_EPS) * g
    return y * (1.0 + scale) + shift


def _pre_kernel(x_ref, mod_ref, g_ref, w_ref, u_ref, z_ref, gate_ref):
    h = _norm_mod(x_ref[...], g_ref[...], mod_ref[0:1, :], mod_ref[1:2, :])
    p = jnp.dot(h.astype(BF16), w_ref[...], preferred_element_type=F32)
    u_ref[...] = p[:, :SSM_WIDTH]
    v1 = p[:, SSM_WIDTH:SSM_WIDTH + CONV_WIDTH]
    v2 = p[:, SSM_WIDTH + CONV_WIDTH:SSM_WIDTH + 2 * CONV_WIDTH]
    z_ref[...] = v1 * jax.nn.sigmoid(v2)
    gate_ref[...] = jax.nn.sigmoid(p[:, SSM_WIDTH + 2 * CONV_WIDTH:]).astype(BF16)


def _pre(xc, mod, g, w_in_bf, n_ctx_tiles):
    n = xc.shape[0]
    nt = n // TM
    kind = lambda i: (jnp.where(i < n_ctx_tiles, 0, 1), 0, 0)
    return pl.pallas_call(
        _pre_kernel,
        grid=(nt,),
        in_specs=[
            pl.BlockSpec((TM, D_MODEL), lambda i: (i, 0)),
            pl.BlockSpec((None, N_MOD, D_MODEL), kind),
            pl.BlockSpec((1, D_MODEL), lambda i: (0, 0)),
            pl.BlockSpec((D_MODEL, IN_COLS), lambda i: (0, 0)),
        ],
        out_specs=[
            pl.BlockSpec((TM, SSM_WIDTH), lambda i: (i, 0)),
            pl.BlockSpec((TM, CONV_WIDTH), lambda i: (i, 0)),
            pl.BlockSpec((TM, 2 * D_MODEL), lambda i: (i, 0)),
        ],
        out_shape=[
            jax.ShapeDtypeStruct((n, SSM_WIDTH), F32),
            jax.ShapeDtypeStruct((n, CONV_WIDTH), F32),
            jax.ShapeDtypeStruct((n, 2 * D_MODEL), BF16),
        ],
        compiler_params=_cparams(("arbitrary",)),
        name="pre",
    )(xc, mod, g, w_in_bf)


def _ssm_kernel(u_ref, bw_ref, cw_ref, ar_ref, ai_ref, pr_ref, pi_ref, y_ref,
                cr_ref, ci_ref, xp_ref, s_ref, yp_ref, *, reverse):
    t = pl.program_id(1)

    @pl.when(t == 0)
    def _():
        cr_ref[...] = jnp.zeros_like(cr_ref)
        ci_ref[...] = jnp.zeros_like(ci_ref)

    def row(i):
        return SEG - 1 - i if reverse else i

    for i in range(SEG):
        xp_ref[SUBLANES * i:SUBLANES * (i + 1), :] = u_ref[pl.ds(row(i), SUBLANES, stride=SEG), :]
    s_ref[...] = jnp.dot(xp_ref[...].astype(BF16), bw_ref[...], preferred_element_type=F32)

    ar = ar_ref[...]
    ai = ai_ref[...]
    hr = jnp.zeros((SUBLANES, SB), F32)
    hi = jnp.zeros((SUBLANES, SB), F32)
    for i in range(SEG):
        rows = slice(SUBLANES * i, SUBLANES * (i + 1))
        nhr = ar * hr - ai * hi + s_ref[rows, 0:SB]
        nhi = ar * hi + ai * hr + s_ref[rows, SB:2 * SB]
        hr, hi = nhr, nhi
        s_ref[rows, 0:SB] = hr
        s_ref[rows, SB:2 * SB] = hi

    asr = pr_ref[SEG - 1:SEG, :]
    asi = pi_ref[SEG - 1:SEG, :]
    pr_ = cr_ref[...]
    pi_ = ci_ref[...]
    prev_r = [None] * SUBLANES
    prev_i = [None] * SUBLANES
    order = range(SUBLANES - 1, -1, -1) if reverse else range(SUBLANES)
    for s in order:
        prev_r[s] = pr_
        prev_i[s] = pi_
        er = hr[s:s + 1, :]
        ei = hi[s:s + 1, :]
        pr_, pi_ = er + asr * pr_ - asi * pi_, ei + asr * pi_ + asi * pr_
    cr_ref[...] = pr_
    ci_ref[...] = pi_
    hpr = jnp.concatenate(prev_r, axis=0)
    hpi = jnp.concatenate(prev_i, axis=0)

    for i in range(SEG):
        rows = slice(SUBLANES * i, SUBLANES * (i + 1))
        qr = pr_ref[i:i + 1, :]
        qi = pi_ref[i:i + 1, :]
        s_ref[rows, 0:SB] = s_ref[rows, 0:SB] + (qr * hpr - qi * hpi)
        s_ref[rows, SB:2 * SB] = s_ref[rows, SB:2 * SB] + (qr * hpi + qi * hpr)

    yp_ref[...] = jnp.dot(s_ref[...].astype(BF16), cw_ref[...], preferred_element_type=F32)
    for i in range(SEG):
        y_ref[pl.ds(row(i), SUBLANES, stride=SEG), :] = yp_ref[SUBLANES * i:SUBLANES * (i + 1), :]


def _ssm(u, bw, cw, ar, ai, pr, pi, n_ctx_tiles, reverse):
    n = u.shape[0]
    nt = n // TM
    if reverse:
        def tile(t):
            return jnp.where(t < n_ctx_tiles, n_ctx_tiles - 1 - t, nt - 1 - (t - n_ctx_tiles))
    else:
        def tile(t):
            return t
    return pl.pallas_call(
        functools.partial(_ssm_kernel, reverse=reverse),
        grid=(KB, nt),
        in_specs=[
            pl.BlockSpec((TM, LANES), lambda k, t: (tile(t), k)),
            pl.BlockSpec((None, LANES, 2 * SB), lambda k, t: (k, 0, 0)),
            pl.BlockSpec((None, 2 * SB, LANES), lambda k, t: (k, 0, 0)),
            pl.BlockSpec((None, 1, SB), lambda k, t: (k, 0, 0)),
            pl.BlockSpec((None, 1, SB), lambda k, t: (k, 0, 0)),
            pl.BlockSpec((None, SEG, SB), lambda k, t: (k, 0, 0)),
            pl.BlockSpec((None, SEG, SB), lambda k, t: (k, 0, 0)),
        ],
        out_specs=pl.BlockSpec((TM, LANES), lambda k, t: (tile(t), k)),
        out_shape=jax.ShapeDtypeStruct((n, SSM_WIDTH), F32),
        scratch_shapes=[
            pltpu.VMEM((1, SB), F32),
            pltpu.VMEM((1, SB), F32),
            pltpu.VMEM((TM, LANES), F32),
            pltpu.VMEM((TM, 2 * SB), F32),
            pltpu.VMEM((TM, LANES), F32),
        ],
        compiler_params=_cparams(("arbitrary", "arbitrary")),
        name="ssm_bwd" if reverse else "ssm_fwd",
    )(u, bw, cw, ar, ai, pr, pi)


def _ssm_params(a_re, a_im, log_dt, b_re, b_im, c_re, c_im):
    lam_re = jnp.minimum(a_re.astype(F32), -1e-4)
    lam_im = a_im.astype(F32)
    dt = jnp.exp(log_dt.astype(F32))[:, None]
    mag = jnp.exp(lam_re * dt)
    abar_re = mag * jnp.cos(lam_im * dt)
    abar_im = mag * jnp.sin(lam_im * dt)
    n_re, n_im = abar_re - 1.0, abar_im
    den = lam_re * lam_re + lam_im * lam_im
    coef_re = (n_re * lam_re + n_im * lam_im) / den
    coef_im = (n_im * lam_re - n_re * lam_im) / den
    br, bi = b_re.astype(F32), b_im.astype(F32)
    bbar_re = coef_re[..., None] * br - coef_im[..., None] * bi
    bbar_im = coef_re[..., None] * bi + coef_im[..., None] * br
    eye = jnp.eye(GPB, dtype=F32)

    def in_block(b):
        b = b.reshape(KB, GPB, SSM_STATE, SSM_GROUP)
        return jnp.einsum('kgpc,gh->kgchp', b, eye).reshape(KB, LANES, SB)

    def out_block(c):
        c = c.reshape(KB, GPB, SSM_GROUP, SSM_STATE)
        return jnp.einsum('kgcp,gh->kgphc', c, eye).reshape(KB, SB, LANES)

    bw = jnp.concatenate([in_block(bbar_re), in_block(bbar_im)], axis=2).astype(BF16)
    cw = jnp.concatenate([out_block(c_re.astype(F32)), -out_block(c_im.astype(F32))], axis=1).astype(BF16)
    ar = abar_re.reshape(KB, 1, SB)
    ai = abar_im.reshape(KB, 1, SB)
    pr = [abar_re]
    pi = [abar_im]
    for _ in range(SEG - 1):
        pr.append(pr[-1] * abar_re - pi[-1] * abar_im)
        pi.append(pr[-2] * abar_im + pi[-1] * abar_re)
    pr = jnp.stack(pr, 0).reshape(SEG, KB, SB).transpose(1, 0, 2)
    pi = jnp.stack(pi, 0).reshape(SEG, KB, SB).transpose(1, 0, 2)
    return bw, cw, ar, ai, pr, pi


def _ln_silu(acc, g, b):
    mu = jnp.mean(acc, axis=-1, keepdims=True)
    xc = acc - mu
    y = xc * lax.rsqrt(jnp.mean(xc * xc, axis=-1, keepdims=True) + NORM_EPS) * g + b
    return y * jax.nn.sigmoid(y)


def _conv_rows_kernel(z_ref, w_ref, cb_ref, g_ref, b_ref, o_ref, zp_ref, *, n_ctx_tiles):
    i = pl.program_id(0)
    w = w_ref[...]
    cb = cb_ref[...]
    g = g_ref[...]
    b = b_ref[...]

    def run(seq):
        nseq = TM // seq
        stride = seq + 2 * PAD
        zeros = jnp.zeros((PAD, CONV_WIDTH), F32)
        for a in range(nseq):
            zp_ref[stride * a:stride * a + PAD, :] = zeros
            zp_ref[stride * a + PAD:stride * a + PAD + seq, :] = z_ref[seq * a:seq * (a + 1), :]
            zp_ref[stride * a + PAD + seq:stride * (a + 1), :] = zeros
        for a in range(nseq):
            acc = jnp.zeros((seq, CONV_WIDTH), F32) + cb
            for d in range(CONV_SIZE):
                start = stride * a + PAD + d - CONV_HALF
                acc = acc + w[d:d + 1, :] * zp_ref[start:start + seq, :]
            o_ref[seq * a:seq * (a + 1), :] = _ln_silu(acc, g, b).astype(o_ref.dtype)

    @pl.when(i < n_ctx_tiles)
    def _():
        run(TM)

    @pl.when(i >= n_ctx_tiles)
    def _():
        run(GRID_W)


def _conv_rows(z, w, cb, g, b, n_ctx_tiles):
    n = z.shape[0]
    nt = n // TM
    vec = pl.BlockSpec((1, CONV_WIDTH), lambda i: (0, 0))
    return pl.pallas_call(
        functools.partial(_conv_rows_kernel, n_ctx_tiles=n_ctx_tiles),
        grid=(nt,),
        in_specs=[
            pl.BlockSpec((TM, CONV_WIDTH), lambda i: (i, 0)),
            pl.BlockSpec((CONV_SIZE, CONV_WIDTH), lambda i: (0, 0)),
            vec, vec, vec,
        ],
        out_specs=pl.BlockSpec((TM, CONV_WIDTH), lambda i: (i, 0)),
        out_shape=jax.ShapeDtypeStruct((n, CONV_WIDTH), BF16),
        scratch_shapes=[pltpu.VMEM((TM // GRID_W * (GRID_W + 2 * PAD), CONV_WIDTH), F32)],
        compiler_params=_cparams(("arbitrary",)),
        name="conv_rows",
    )(z, w, cb, g, b)


WB = 16


def _conv_cols_kernel(z_ref, w_ref, cb_ref, g_ref, b_ref, o_ref, zp_ref, *, r_ctx, rows):
    w = w_ref[...]
    cb = cb_ref[...]
    g = g_ref[...]
    b = b_ref[...]
    zeros = jnp.zeros((CONV_HALF, WB, CONV_WIDTH), F32)
    zp_ref[0:CONV_HALF] = zeros
    zp_ref[CONV_HALF + rows:CONV_HALF + rows + CONV_HALF] = zeros
    zp_ref[CONV_HALF:CONV_HALF + rows] = z_ref[r_ctx:r_ctx + rows]
    o_ref[0:r_ctx] = jnp.zeros((r_ctx, WB, CONV_WIDTH), o_ref.dtype)

    def body(r, carry):
        acc = jnp.zeros((WB, CONV_WIDTH), F32) + cb
        for d in range(CONV_SIZE):
            acc = acc + w[d:d + 1, :] * zp_ref[r + d]
        o_ref[r_ctx + r] = _ln_silu(acc, g, b).astype(o_ref.dtype)
        return carry

    lax.fori_loop(0, rows, body, 0)


def _conv_cols(z, w, cb, g, b, n_ctx):
    n = z.shape[0]
    r_all = n // GRID_W
    r_ctx = n_ctx // GRID_W
    rows = r_all - r_ctx
    z3 = z.reshape(r_all, GRID_W, CONV_WIDTH)
    vec = pl.BlockSpec((1, CONV_WIDTH), lambda j: (0, 0))
    out = pl.pallas_call(
        functools.partial(_conv_cols_kernel, r_ctx=r_ctx, rows=rows),
        grid=(GRID_W // WB,),
        in_specs=[
            pl.BlockSpec((r_all, WB, CONV_WIDTH), lambda j: (0, j, 0)),
            pl.BlockSpec((CONV_SIZE, CONV_WIDTH), lambda j: (0, 0)),
            vec, vec, vec,
        ],
        out_specs=pl.BlockSpec((r_all, WB, CONV_WIDTH), lambda j: (0, j, 0)),
        out_shape=jax.ShapeDtypeStruct((r_all, GRID_W, CONV_WIDTH), BF16),
        scratch_shapes=[pltpu.VMEM((rows + 2 * CONV_HALF, WB, CONV_WIDTH), F32)],
        compiler_params=_cparams(("arbitrary",)),
        name="conv_cols",
    )(z3, w, cb, g, b)
    return out.reshape(n, CONV_WIDTH)


def _route(lt):
    l = [lt[e:e + 1, :] for e in range(N_EXPERTS)]
    m = l[0]
    for e in range(1, N_EXPERTS):
        m = jnp.maximum(m, l[e])
    ex = [jnp.exp(v - m) for v in l]
    den = ex[0]
    for e in range(1, N_EXPERTS):
        den = den + ex[e]
    p = [v / den for v in ex]
    scores = []
    for gidx in range(N_EXPERT_GROUPS):
        q = p[gidx * EXPERTS_PER_GROUP:(gidx + 1) * EXPERTS_PER_GROUP]
        best = None
        for a in range(EXPERTS_PER_GROUP):
            for c in range(a + 1, EXPERTS_PER_GROUP):
                pair = q[a] + q[c]
                best = pair if best is None else jnp.maximum(best, pair)
        scores.append(best)
    bs = scores[0]
    bg = jnp.zeros_like(bs)
    for gidx in range(1, N_EXPERT_GROUPS):
        better = scores[gidx] > bs
        bs = jnp.where(better, scores[gidx], bs)
        bg = jnp.where(better, float(gidx), bg)
    q = []
    for a in range(EXPERTS_PER_GROUP):
        v = p[a]
        for gidx in range(1, N_EXPERT_GROUPS):
            v = jnp.where(bg == float(gidx), p[gidx * EXPERTS_PER_GROUP + a], v)
        q.append(v)
    p1 = q[0]
    i1 = jnp.zeros_like(p1)
    for a in range(1, EXPERTS_PER_GROUP):
        better = q[a] > p1
        p1 = jnp.where(better, q[a], p1)
        i1 = jnp.where(better, float(a), i1)
    p2 = jnp.full_like(p1, -1.0)
    i2 = jnp.zeros_like(p1)
    for a in range(EXPERTS_PER_GROUP):
        better = (q[a] > p2) & (i1 != float(a))
        p2 = jnp.where(better, q[a], p2)
        i2 = jnp.where(better, float(a), i2)
    tot = p1 + p2
    base = bg * float(EXPERTS_PER_GROUP)
    return base + i1, base + i2, p1 / tot, p2 / tot


def _merge_kernel(x_ref, u_ref, yf_ref, yb_ref, bp_ref, gate_ref, mod_ref, d_ref, wa_ref, wb_ref,
                  wo_ref, gf_ref, rwh_ref, rwl_ref, rb_ref, xo_ref, h2_ref, r_ref):
    ya = jax.nn.gelu(yf_ref[...] + yb_ref[...] + d_ref[...] * u_ref[...], approximate=True)
    av = jnp.dot(ya.astype(BF16), wa_ref[...], preferred_element_type=F32)
    a_out = av[:, :D_MODEL] * jax.nn.sigmoid(av[:, D_MODEL:])
    b_out = jnp.dot(bp_ref[...], wb_ref[...], preferred_element_type=F32)
    gate = gate_ref[...].astype(F32)
    merged = gate[:, :D_MODEL] * a_out + gate[:, D_MODEL:] * b_out
    mx = jnp.dot(merged.astype(BF16), wo_ref[...], preferred_element_type=F32)
    xn = x_ref[...] + mod_ref[2:3, :] * mx
    xo_ref[...] = xn
    h2 = _norm_mod(xn, gf_ref[...], mod_ref[3:4, :], mod_ref[4:5, :])
    h2_ref[...] = h2.astype(BF16)
    h_hi, h_lo = _split_bf16(h2)
    nt_dims = (((1,), (1,)), ((), ()))
    lt = lax.dot_general(rwh_ref[...], h_hi, nt_dims, preferred_element_type=F32)
    lt += lax.dot_general(rwh_ref[...], h_lo, nt_dims, preferred_element_type=F32)
    lt += lax.dot_general(rwl_ref[...], h_hi, nt_dims, preferred_element_type=F32)
    lt = lt + rb_ref[...]
    e1, e2, w1, w2 = _route(lt)
    zero = jnp.zeros_like(e1)
    r_ref[...] = jnp.concatenate([e1, e2, w1, w2, zero, zero, zero, zero], axis=0)


def _merge(xc, u, yf, yb, bp, gates, mod, d_skip, wa, wb, wo, gf, rwh, rwl, rb, n_ctx_tiles):
    n = xc.shape[0]
    nt = n // TM
    kind = lambda i: (jnp.where(i < n_ctx_tiles, 0, 1), 0, 0)
    tile = lambda w: pl.BlockSpec((TM, w), lambda i: (i, 0))
    full = lambda a: pl.BlockSpec(a.shape, lambda i: (0,) * a.ndim)
    return pl.pallas_call(
        _merge_kernel,
        grid=(nt,),
        in_specs=[
            tile(D_MODEL), tile(SSM_WIDTH), tile(SSM_WIDTH), tile(SSM_WIDTH), tile(CONV_WIDTH),
            tile(2 * D_MODEL),
            pl.BlockSpec((None, N_MOD, D_MODEL), kind),
            full(d_skip), full(wa), full(wb), full(wo), full(gf), full(rwh), full(rwl), full(rb),
        ],
        out_specs=[
            tile(D_MODEL), tile(D_MODEL),
            pl.BlockSpec((SUBLANES, TM), lambda i: (0, i)),
        ],
        out_shape=[
            jax.ShapeDtypeStruct((n, D_MODEL), F32),
            jax.ShapeDtypeStruct((n, D_MODEL), BF16),
            jax.ShapeDtypeStruct((SUBLANES, n), F32),
        ],
        compiler_params=_cparams(("arbitrary",)),
        name="merge",
    )(xc, u, yf, yb, bp, gates, mod, d_skip, wa, wb, wo, gf, rwh, rwl, rb)


def _moe_dense_kernel(x_ref, h_ref, rt_ref, mod_ref, wg_ref, wu_ref, wd_ref, o_ref, *, n_ctx):
    e = pl.program_id(1)

    @pl.when(e == 0)
    def _():
        o_ref[...] = x_ref[...]

    h = h_ref[...]
    hid = jnp.dot(h, wg_ref[...], preferred_element_type=F32)
    hid = hid * jax.nn.sigmoid(hid) * jnp.dot(h, wu_ref[...], preferred_element_type=F32)
    y = jnp.dot(hid.astype(BF16), wd_ref[...], preferred_element_type=F32)
    rt = rt_ref[...]
    ef = e.astype(F32)
    dw = jnp.where(rt[:, 0:1] == ef, rt[:, 2:3], 0.0) + jnp.where(rt[:, 1:2] == ef, rt[:, 3:4], 0.0)
    row = pl.program_id(0) * o_ref.shape[0] + lax.broadcasted_iota(jnp.int32, (o_ref.shape[0], 1), 0)
    m5 = jnp.where(row < n_ctx, mod_ref[0, 5:6, :], mod_ref[1, 5:6, :])
    o_ref[...] += m5 * (dw * y)


def _moe_dense(xn, h2, rt, mod, wg, wu, wd, n_ctx_tiles, tmm):
    n = xn.shape[0]
    nt = n // tmm
    return pl.pallas_call(
        functools.partial(_moe_dense_kernel, n_ctx=n_ctx_tiles * TM),
        grid=(nt, N_EXPERTS),
        in_specs=[
            pl.BlockSpec((tmm, D_MODEL), lambda i, e: (i, 0)),
            pl.BlockSpec((tmm, D_MODEL), lambda i, e: (i, 0)),
            pl.BlockSpec((tmm, SUBLANES), lambda i, e: (i, 0)),
            pl.BlockSpec((2, N_MOD, D_MODEL), lambda i, e: (0, 0, 0)),
            pl.BlockSpec((None, D_MODEL, D_EXPERT), lambda i, e: (e, 0, 0)),
            pl.BlockSpec((None, D_MODEL, D_EXPERT), lambda i, e: (e, 0, 0)),
            pl.BlockSpec((None, D_EXPERT, D_MODEL), lambda i, e: (e, 0, 0)),
        ],
        out_specs=pl.BlockSpec((tmm, D_MODEL), lambda i, e: (i, 0)),
        out_shape=jax.ShapeDtypeStruct((n, D_MODEL), F32),
        compiler_params=_cparams(("arbitrary", "arbitrary")),
        name="moe_dense",
    )(xn, h2, rt, mod, wg, wu, wd)


def _final_kernel(x_ref, g_ref, o_ref):
    x = x_ref[...]
    o_ref[...] = x * lax.rsqrt(jnp.mean(x * x, axis=-1, keepdims=True) + NORM_EPS) * g_ref[...]


def _final(xc, g, n_ctx_tiles):
    n = xc.shape[0] - n_ctx_tiles * TM
    return pl.pallas_call(
        _final_kernel,
        grid=(n // TM,),
        in_specs=[
            pl.BlockSpec((TM, D_MODEL), lambda i: (i + n_ctx_tiles, 0)),
            pl.BlockSpec((1, D_MODEL), lambda i: (0, 0)),
        ],
        out_specs=pl.BlockSpec((TM, D_MODEL), lambda i: (i, 0)),
        out_shape=jax.ShapeDtypeStruct((n, D_MODEL), F32),
        compiler_params=_cparams(("arbitrary",)),
        name="final_norm",
    )(xc, g)


def kernel(x, c, ctx, c_ctx, ada_w, ada_b, norm_mix_g, norm_ffn_g, w_in, ssm_a_re, ssm_a_im, ssm_log_dt, ssm_b_re, ssm_b_im, ssm_c_re, ssm_c_im, ssm_d, w_a_val, w_a_gate, conv_w, conv_b, conv_ln_g, conv_ln_b, w_b_out, w_out, router_w, router_b, w_exp_gate, w_exp_up, w_exp_down, final_norm_g):
    depth = ada_w.shape[0]
    bsz, seq, _ = x.shape
    n_ctx = ctx.shape[1]
    assert bsz == 1 and n_ctx == TM and seq % TM == 0 and seq % GRID_W == 0
    n_ctx_tiles = n_ctx // TM
    n = n_ctx + seq

    xc = jnp.concatenate([ctx[0], x[0]], axis=0)
    cc = jnp.zeros((SUBLANES, D_MODEL), F32).at[0].set(c_ctx).at[1].set(c[0])
    mods = _mods(cc, ada_w, ada_b)[:, :2].reshape(depth, 2, N_MOD, D_MODEL)

    rwt = router_w.T
    rwh, rwl = _split_bf16(rwt)
    rb = router_b.reshape(N_EXPERTS, 1)
    tmm = 1280 if n % 1280 == 0 else TM

    for l in range(depth):
        last = l == depth - 1
        u, z, gates = _pre(xc, mods[l], norm_mix_g[l].reshape(1, -1), w_in[l].astype(BF16), n_ctx_tiles)
        ys = []
        for k in range(2):
            prm = _ssm_params(ssm_a_re[l, k], ssm_a_im[l, k], ssm_log_dt[l, k], ssm_b_re[l, k],
                              ssm_b_im[l, k], ssm_c_re[l, k], ssm_c_im[l, k])
            ys.append(_ssm(u, *prm, n_ctx_tiles, reverse=(k == 1)))
        vec = lambda a: a.reshape(1, -1)
        conv_args = (conv_w[l], vec(conv_b[l]), vec(conv_ln_g[l]), vec(conv_ln_b[l]))
        if l % 2 == 0:
            bp = _conv_rows(z, *conv_args, n_ctx_tiles)
        else:
            assert last, "column conv over the context prefix is not needed by the last layer only"
            bp = _conv_cols(z, *conv_args, n_ctx)
        wa = jnp.concatenate([w_a_val[l], w_a_gate[l]], axis=1).astype(BF16)
        xn, h2, r = _merge(xc, u, ys[0], ys[1], bp, gates, mods[l], vec(ssm_d[l]), wa,
                           w_b_out[l].astype(BF16), w_out[l].astype(BF16), vec(norm_ffn_g[l]),
                           rwh, rwl, rb, n_ctx_tiles)
        xc = _moe_dense(xn, h2, r.T, mods[l], w_exp_gate[l].astype(BF16), w_exp_up[l].astype(BF16),
                        w_exp_down[l].astype(BF16), n_ctx_tiles, tmm)
    return _final(xc, vec(final_norm_g), n_ctx_tiles)[None]
```

```python
import functools
import math

import jax
import jax.numpy as jnp
from jax import lax
from jax.experimental import pallas as pl
from jax.experimental.pallas import tpu as pltpu

F32 = jnp.float32
BF16 = jnp.bfloat16

D_MODEL = 1024
GRID_W = 64
SSM_WIDTH = 512
SSM_GROUP = 16
SSM_GROUPS = SSM_WIDTH // SSM_GROUP
SSM_STATE = 64
CONV_WIDTH = 512
CONV_SIZE = 31
CONV_HALF = CONV_SIZE // 2
IN_COLS = SSM_WIDTH + 2 * CONV_WIDTH + 2 * D_MODEL
N_EXPERTS = 16
N_EXPERT_GROUPS = 4
EXPERTS_PER_GROUP = N_EXPERTS // N_EXPERT_GROUPS
D_EXPERT = 512
N_MOD = 6
NORM_EPS = 1e-6

TM = 256
LANES = 128
SUBLANES = 8
SEG = TM // SUBLANES
KB = SSM_WIDTH // LANES
GPB = LANES // SSM_GROUP
SB = GPB * SSM_STATE
PAD = 16
CHUNK = 32
VMEM_LIMIT = 56 * 1024 * 1024


def _cparams(sem):
    return pltpu.CompilerParams(dimension_semantics=sem, vmem_limit_bytes=VMEM_LIMIT)


def _split_bf16(a):
    hi = a.astype(BF16)
    lo = (a - hi.astype(F32)).astype(BF16)
    return hi, lo


def _mods_kernel(c_ref, w_ref, b_ref, o_ref):
    s = c_ref[...]
    s = s * jax.nn.sigmoid(s)
    s_hi, s_lo = _split_bf16(s)
    w_hi, w_lo = _split_bf16(w_ref[...])
    acc = jnp.dot(s_hi, w_hi, preferred_element_type=F32)
    acc += jnp.dot(s_lo, w_hi, preferred_element_type=F32)
    acc += jnp.dot(s_hi, w_lo, preferred_element_type=F32)
    o_ref[...] = acc + b_ref[...]


def _mods(cc, ada_w, ada_b):
    depth = ada_w.shape[0]
    cols = ada_w.shape[2]
    cb = 1536
    return pl.pallas_call(
        _mods_kernel,
        grid=(depth, cols // cb),
        in_specs=[
            pl.BlockSpec((SUBLANES, D_MODEL), lambda l, j: (0, 0)),
            pl.BlockSpec((None, D_MODEL, cb), lambda l, j: (l, 0, j)),
            pl.BlockSpec((None, 1, cb), lambda l, j: (l, 0, j)),
        ],
        out_specs=pl.BlockSpec((None, SUBLANES, cb), lambda l, j: (l, 0, j)),
        out_shape=jax.ShapeDtypeStruct((depth, SUBLANES, cols), F32),
        compiler_params=_cparams(("arbitrary", "arbitrary")),
        name="mods",
    )(cc, ada_w, ada_b.reshape(depth, 1, cols))


def _norm_mod(x, g, shift, scale):
    y = x * lax.rsqrt(jnp.mean(x * x, axis=-1, keepdims=True) + NORM_EPS) * g
    return y * (1.0 + scale) + shift


def _pre_kernel(x_ref, mod_ref, g_ref, w_ref, pm_ref, u_ref, z_ref, gate_ref):
    h = _norm_mod(x_ref[...], g_ref[...], mod_ref[0:1, :], mod_ref[1:2, :])
    p = jnp.dot(h.astype(BF16), w_ref[...], preferred_element_type=F32)
    u_ref[...] = jnp.dot(pm_ref[...], p[:, :SSM_WIDTH].astype(BF16),
                         preferred_element_type=F32).astype(BF16)
    v1 = p[:, SSM_WIDTH:SSM_WIDTH + CONV_WIDTH]
    v2 = p[:, SSM_WIDTH + CONV_WIDTH:SSM_WIDTH + 2 * CONV_WIDTH]
    z_ref[...] = v1 * jax.nn.sigmoid(v2)
    gate_ref[...] = jax.nn.sigmoid(p[:, SSM_WIDTH + 2 * CONV_WIDTH:]).astype(BF16)


def _pre(xc, mod, g, w_in_bf, pm, n_ctx_tiles):
    n = xc.shape[0]
    nt = n // TM
    kind = lambda i: (jnp.where(i < n_ctx_tiles, 0, 1), 0, 0)
    return pl.pallas_call(
        _pre_kernel,
        grid=(nt,),
        in_specs=[
            pl.BlockSpec((TM, D_MODEL), lambda i: (i, 0)),
            pl.BlockSpec((None, N_MOD, D_MODEL), kind),
            pl.BlockSpec((1, D_MODEL), lambda i: (0, 0)),
            pl.BlockSpec((D_MODEL, IN_COLS), lambda i: (0, 0)),
            pl.BlockSpec((TM, TM), lambda i: (0, 0)),
        ],
        out_specs=[
            pl.BlockSpec((TM, SSM_WIDTH), lambda i: (i, 0)),
            pl.BlockSpec((TM, CONV_WIDTH), lambda i: (i, 0)),
            pl.BlockSpec((TM, 2 * D_MODEL), lambda i: (i, 0)),
        ],
        out_shape=[
            jax.ShapeDtypeStruct((n, SSM_WIDTH), BF16),
            jax.ShapeDtypeStruct((n, CONV_WIDTH), F32),
            jax.ShapeDtypeStruct((n, 2 * D_MODEL), BF16),
        ],
        compiler_params=_cparams(("arbitrary",)),
        name="pre",
    )(xc, mod, g, w_in_bf, pm)


def _ssm_kernel(uf_ref, ub_ref, bw_ref, cw_ref, ar_ref, ai_ref, qr_ref, qi_ref,
                yf_ref, yb_ref, cr_ref, ci_ref, sf_ref, sb_ref, hf_ref, hb_ref):
    @pl.when(pl.program_id(1) == 0)
    def _():
        cr_ref[...] = jnp.zeros_like(cr_ref)
        ci_ref[...] = jnp.zeros_like(ci_ref)

    dirs = (0, 1)
    u_refs = (uf_ref, ub_ref)
    y_refs = (yf_ref, yb_ref)
    s_refs = (sf_ref, sb_ref)
    for d in dirs:
        s_refs[d][...] = jnp.dot(u_refs[d][...], bw_ref[d], preferred_element_type=F32)

    ar = [ar_ref[d] for d in dirs]
    ai = [ai_ref[d] for d in dirs]

    def group(d, i):
        return SEG - 1 - i if d == 1 else i

    def advance(carry, i):
        out = []
        for d in dirs:
            rows = pl.ds(pl.multiple_of(group(d, i) * SUBLANES, SUBLANES), SUBLANES)
            hr_d, hi_d = carry[2 * d], carry[2 * d + 1]
            out.append(ar[d] * hr_d - ai[d] * hi_d + s_refs[d][rows, 0:SB])
            out.append(ar[d] * hi_d + ai[d] * hr_d + s_refs[d][rows, SB:2 * SB])
        return tuple(out)

    zero = jnp.zeros((SUBLANES, SB), F32)
    ends = lax.fori_loop(0, SEG, lambda i, c: advance(c, i), (zero,) * 4, unroll=2)
    hr = [ends[0], ends[2]]
    hi = [ends[1], ends[3]]

    hpr = [None, None]
    hpi = [None, None]
    for d in dirs:
        asr = qr_ref[d, 0:1, :]
        asi = qi_ref[d, 0:1, :]
        pr_ = cr_ref[d]
        pi_ = ci_ref[d]
        prev_r = [None] * SUBLANES
        prev_i = [None] * SUBLANES
        order = range(SUBLANES - 1, -1, -1) if d == 1 else range(SUBLANES)
        for s in order:
            prev_r[s] = pr_
            prev_i[s] = pi_
            er = hr[d][s:s + 1, :]
            ei = hi[d][s:s + 1, :]
            pr_, pi_ = er + asr * pr_ - asi * pi_, ei + asr * pi_ + asi * pr_
        cr_ref[d] = pr_
        ci_ref[d] = pi_
        hpr[d] = jnp.concatenate(prev_r, axis=0)
        hpi[d] = jnp.concatenate(prev_i, axis=0)

    h_refs = (hf_ref, hb_ref)

    def rescan(j, carry):
        first = advance(carry, 2 * j)
        second = advance(first, 2 * j + 1)
        for d in dirs:
            lo, hi_ = (first, second) if d == 0 else (second, first)
            pair = j if d == 0 else SEG // 2 - 1 - j
            rows = pl.ds(pl.multiple_of(pair * 2 * SUBLANES, 2 * SUBLANES), 2 * SUBLANES)
            h_refs[d][rows, 0:SB] = jnp.concatenate([lo[2 * d], hi_[2 * d]], axis=0).astype(BF16)
            h_refs[d][rows, SB:2 * SB] = jnp.concatenate(
                [lo[2 * d + 1], hi_[2 * d + 1]], axis=0).astype(BF16)
        return second

    lax.fori_loop(0, SEG // 2, rescan, (hpr[0], hpi[0], hpr[1], hpi[1]))

    for d in dirs:
        y_refs[d][...] = jnp.dot(h_refs[d][...], cw_ref[d], preferred_element_type=F32)


def _ssm(u, bw, cw, ar, ai, qr, qi, n_ctx_tiles):
    n = u.shape[0]
    nt = n // TM

    def fwd(k, t):
        return (t, k)

    def bwd(k, t):
        return (jnp.where(t < n_ctx_tiles, n_ctx_tiles - 1 - t, nt - 1 - (t - n_ctx_tiles)), k)

    per_k = lambda *shape: pl.BlockSpec((2, None) + shape, lambda k, t: (0, k) + (0,) * len(shape))
    return pl.pallas_call(
        _ssm_kernel,
        grid=(KB, nt),
        in_specs=[
            pl.BlockSpec((TM, LANES), fwd),
            pl.BlockSpec((TM, LANES), bwd),
            per_k(LANES, 2 * SB),
            per_k(2 * SB, LANES),
            per_k(SUBLANES, SB),
            per_k(SUBLANES, SB),
            per_k(SUBLANES, SB),
            per_k(SUBLANES, SB),
        ],
        out_specs=[pl.BlockSpec((TM, LANES), fwd), pl.BlockSpec((TM, LANES), bwd)],
        out_shape=[jax.ShapeDtypeStruct((n, SSM_WIDTH), F32)] * 2,
        scratch_shapes=[
            pltpu.VMEM((2, 1, SB), F32),
            pltpu.VMEM((2, 1, SB), F32),
            pltpu.VMEM((TM, 2 * SB), F32),
            pltpu.VMEM((TM, 2 * SB), F32),
            pltpu.VMEM((TM, 2 * SB), BF16),
            pltpu.VMEM((TM, 2 * SB), BF16),
        ],
        compiler_params=_cparams(("arbitrary", "arbitrary")),
        name="ssm",
    )(u, u, bw, cw, ar, ai, qr, qi)


def _ssm_perm():
    p = jnp.arange(TM)
    src = SEG * (p % SUBLANES) + p // SUBLANES
    pm = (src[:, None] == jnp.arange(TM)[None, :]).astype(BF16)
    return pm, pm.T


def _ssm_params(a_re, a_im, log_dt, b_re, b_im, c_re, c_im):
    lam_re = jnp.minimum(a_re.astype(F32), -1e-4)
    lam_im = a_im.astype(F32)
    dt = jnp.exp(log_dt.astype(F32))[:, None]
    mag = jnp.exp(lam_re * dt)
    abar_re = mag * jnp.cos(lam_im * dt)
    abar_im = mag * jnp.sin(lam_im * dt)
    n_re, n_im = abar_re - 1.0, abar_im
    den = lam_re * lam_re + lam_im * lam_im
    coef_re = (n_re * lam_re + n_im * lam_im) / den
    coef_im = (n_im * lam_re - n_re * lam_im) / den
    br, bi = b_re.astype(F32), b_im.astype(F32)
    bbar_re = coef_re[..., None] * br - coef_im[..., None] * bi
    bbar_im = coef_re[..., None] * bi + coef_im[..., None] * br
    eye = jnp.eye(GPB, dtype=F32)

    def in_block(b):
        b = b.reshape(KB, GPB, SSM_STATE, SSM_GROUP)
        return jnp.einsum('kgpc,gh->kgchp', b, eye).reshape(KB, LANES, SB)

    def out_block(c):
        c = c.reshape(KB, GPB, SSM_GROUP, SSM_STATE)
        return jnp.einsum('kgcp,gh->kgphc', c, eye).reshape(KB, SB, LANES)

    bw = jnp.concatenate([in_block(bbar_re), in_block(bbar_im)], axis=2).astype(BF16)
    cw = jnp.concatenate([out_block(c_re.astype(F32)), -out_block(c_im.astype(F32))], axis=1).astype(BF16)
    qr, qi = abar_re, abar_im
    for _ in range(SEG - 1):
        qr, qi = qr * abar_re - qi * abar_im, qr * abar_im + qi * abar_re
    rep = lambda a: jnp.broadcast_to(a.reshape(KB, 1, SB), (KB, SUBLANES, SB))
    return bw, cw, rep(abar_re), rep(abar_im), rep(qr), rep(qi)


def _ln_silu(acc, g, b):
    mu = jnp.mean(acc, axis=-1, keepdims=True)
    xc = acc - mu
    y = xc * lax.rsqrt(jnp.mean(xc * xc, axis=-1, keepdims=True) + NORM_EPS) * g + b
    return y * jax.nn.sigmoid(y)


def _conv_rows_kernel(z_ref, w_ref, cb_ref, g_ref, b_ref, o_ref, zp_ref, *, n_ctx_tiles):
    i = pl.program_id(0)
    w = w_ref[...]
    cb = cb_ref[...]
    g = g_ref[...]
    b = b_ref[...]

    def run(seq):
        nseq = TM // seq
        stride = seq + 2 * PAD
        total = nseq * stride
        zeros = jnp.zeros((PAD, CONV_WIDTH), F32)
        for a in range(nseq):
            zp_ref[0, stride * a:stride * a + PAD, :] = zeros
            zp_ref[0, stride * a + PAD:stride * a + PAD + seq, :] = z_ref[seq * a:seq * (a + 1), :]
            zp_ref[0, stride * a + PAD + seq:stride * (a + 1), :] = zeros
        for k in range(1, SUBLANES):
            for r0 in range(0, total - SUBLANES, CHUNK):
                rows = min(CHUNK, total - SUBLANES - r0)
                zp_ref[k, r0:r0 + rows, :] = zp_ref[0, r0 + k:r0 + k + rows, :]
        for a in range(nseq):
            for r0 in range(0, seq, CHUNK):
                acc = jnp.zeros((CHUNK, CONV_WIDTH), F32) + cb
                for d in range(CONV_SIZE):
                    off = PAD + d - CONV_HALF
                    start = stride * a + r0 + off // SUBLANES * SUBLANES
                    acc = acc + w[d:d + 1, :] * zp_ref[off % SUBLANES, start:start + CHUNK, :]
                o_ref[seq * a + r0:seq * a + r0 + CHUNK, :] = _ln_silu(acc, g, b).astype(o_ref.dtype)

    @pl.when(i < n_ctx_tiles)
    def _():
        run(TM)

    @pl.when(i >= n_ctx_tiles)
    def _():
        run(GRID_W)


def _conv_rows(z, w, cb, g, b, n_ctx_tiles):
    n = z.shape[0]
    nt = n // TM
    vec = pl.BlockSpec((1, CONV_WIDTH), lambda i: (0, 0))
    return pl.pallas_call(
        functools.partial(_conv_rows_kernel, n_ctx_tiles=n_ctx_tiles),
        grid=(nt,),
        in_specs=[
            pl.BlockSpec((TM, CONV_WIDTH), lambda i: (i, 0)),
            pl.BlockSpec((CONV_SIZE, CONV_WIDTH), lambda i: (0, 0)),
            vec, vec, vec,
        ],
        out_specs=pl.BlockSpec((TM, CONV_WIDTH), lambda i: (i, 0)),
        out_shape=jax.ShapeDtypeStruct((n, CONV_WIDTH), BF16),
        scratch_shapes=[pltpu.VMEM((SUBLANES, TM // GRID_W * (GRID_W + 2 * PAD), CONV_WIDTH), F32)],
        compiler_params=_cparams(("arbitrary",)),
        name="conv_rows",
    )(z, w, cb, g, b)


WB = 16
RB = 16


def _conv_cols_kernel(z_ref, w_ref, cb_ref, g_ref, b_ref, o_ref, zp_ref, *, r_ctx, rows):
    w = w_ref[...]
    cb = cb_ref[...]
    g = g_ref[...]
    b = b_ref[...]
    zeros = jnp.zeros((CONV_HALF, WB, CONV_WIDTH), F32)
    zp_ref[0:CONV_HALF] = zeros
    zp_ref[CONV_HALF + rows:CONV_HALF + rows + CONV_HALF] = zeros
    zp_ref[CONV_HALF:CONV_HALF + rows] = z_ref[r_ctx:r_ctx + rows]
    o_ref[0:r_ctx] = jnp.zeros((r_ctx, WB, CONV_WIDTH), o_ref.dtype)

    def column(h, c):
        hs = slice(SUBLANES * h, SUBLANES * (h + 1))
        ls = slice(LANES * c, LANES * (c + 1))
        wv = [jnp.broadcast_to(w[d:d + 1, ls], (SUBLANES, LANES)) for d in range(CONV_SIZE)]
        bias = jnp.broadcast_to(cb[:, ls], (SUBLANES, LANES))

        def body(blk, carry):
            r0 = blk * RB
            acc = [bias] * RB
            for q in range(RB + CONV_SIZE - 1):
                zq = zp_ref[r0 + q, hs, ls]
                for j in range(RB):
                    d = q - j
                    if 0 <= d < CONV_SIZE:
                        acc[j] = acc[j] + wv[d] * zq
            for j in range(RB):
                zp_ref[r0 + j, hs, ls] = acc[j]
            return carry

        lax.fori_loop(0, rows // RB, body, 0)

    for h in range(WB // SUBLANES):
        for c in range(CONV_WIDTH // LANES):
            column(h, c)

    def norm(blk, carry):
        r0 = blk * SUBLANES
        v = zp_ref[pl.ds(r0, SUBLANES)].reshape(SUBLANES * WB, CONV_WIDTH)
        y = _ln_silu(v, g, b).astype(o_ref.dtype)
        o_ref[pl.ds(r_ctx + r0, SUBLANES)] = y.reshape(SUBLANES, WB, CONV_WIDTH)
        return carry

    lax.fori_loop(0, rows // SUBLANES, norm, 0)


def _conv_cols(z, w, cb, g, b, n_ctx):
    n = z.shape[0]
    r_all = n // GRID_W
    r_ctx = n_ctx // GRID_W
    rows = r_all - r_ctx
    z3 = z.reshape(r_all, GRID_W, CONV_WIDTH)
    vec = pl.BlockSpec((1, CONV_WIDTH), lambda j: (0, 0))
    out = pl.pallas_call(
        functools.partial(_conv_cols_kernel, r_ctx=r_ctx, rows=rows),
        grid=(GRID_W // WB,),
        in_specs=[
            pl.BlockSpec((r_all, WB, CONV_WIDTH), lambda j: (0, j, 0)),
            pl.BlockSpec((CONV_SIZE, CONV_WIDTH), lambda j: (0, 0)),
            vec, vec, vec,
        ],
        out_specs=pl.BlockSpec((r_all, WB, CONV_WIDTH), lambda j: (0, j, 0)),
        out_shape=jax.ShapeDtypeStruct((r_all, GRID_W, CONV_WIDTH), BF16),
        scratch_shapes=[pltpu.VMEM((rows + 2 * CONV_HALF, WB, CONV_WIDTH), F32)],
        compiler_params=_cparams(("arbitrary",)),
        name="conv_cols",
    )(z3, w, cb, g, b)
    return out.reshape(n, CONV_WIDTH)


def _route(lt):
    l = [lt[e:e + 1, :] for e in range(N_EXPERTS)]
    m = l[0]
    for e in range(1, N_EXPERTS):
        m = jnp.maximum(m, l[e])
    ex = [jnp.exp(v - m) for v in l]
    den = ex[0]
    for e in range(1, N_EXPERTS):
        den = den + ex[e]
    p = [v / den for v in ex]
    scores = []
    for gidx in range(N_EXPERT_GROUPS):
        q = p[gidx * EXPERTS_PER_GROUP:(gidx + 1) * EXPERTS_PER_GROUP]
        best = None
        for a in range(EXPERTS_PER_GROUP):
            for c in range(a + 1, EXPERTS_PER_GROUP):
                pair = q[a] + q[c]
                best = pair if best is None else jnp.maximum(best, pair)
        scores.append(best)
    bs = scores[0]
    bg = jnp.zeros_like(bs)
    for gidx in range(1, N_EXPERT_GROUPS):
        better = scores[gidx] > bs
        bs = jnp.where(better, scores[gidx], bs)
        bg = jnp.where(better, float(gidx), bg)
    q = []
    for a in range(EXPERTS_PER_GROUP):
        v = p[a]
        for gidx in range(1, N_EXPERT_GROUPS):
            v = jnp.where(bg == float(gidx), p[gidx * EXPERTS_PER_GROUP + a], v)
        q.append(v)
    p1 = q[0]
    i1 = jnp.zeros_like(p1)
    for a in range(1, EXPERTS_PER_GROUP):
        better = q[a] > p1
        p1 = jnp.where(better, q[a], p1)
        i1 = jnp.where(better, float(a), i1)
    p2 = jnp.full_like(p1, -1.0)
    i2 = jnp.zeros_like(p1)
    for a in range(EXPERTS_PER_GROUP):
        better = (q[a] > p2) & (i1 != float(a))
        p2 = jnp.where(better, q[a], p2)
        i2 = jnp.where(better, float(a), i2)
    tot = p1 + p2
    base = bg * float(EXPERTS_PER_GROUP)
    return base + i1, base + i2, p1 / tot, p2 / tot


def _merge_kernel(x_ref, u_ref, yf_ref, yb_ref, bp_ref, gate_ref, mod_ref, d_ref, pmt_ref, wa_ref, wb_ref,
                  wo_ref, gf_ref, rwh_ref, rwl_ref, rb_ref, xo_ref, h2_ref, r_ref, c_ref, cnt_ref):
    ya = jax.nn.gelu(yf_ref[...] + yb_ref[...] + d_ref[...] * u_ref[...].astype(F32), approximate=True)
    ya = jnp.dot(pmt_ref[...], ya.astype(BF16), preferred_element_type=F32).astype(BF16)
    av = jnp.dot(ya, wa_ref[...], preferred_element_type=F32)
    a_out = av[:, :D_MODEL] * jax.nn.sigmoid(av[:, D_MODEL:])
    b_out = jnp.dot(bp_ref[...], wb_ref[...], preferred_element_type=F32)
    gate = gate_ref[...].astype(F32)
    merged = gate[:, :D_MODEL] * a_out + gate[:, D_MODEL:] * b_out
    mx = jnp.dot(merged.astype(BF16), wo_ref[...], preferred_element_type=F32)
    xn = x_ref[...] + mod_ref[2:3, :] * mx
    xo_ref[...] = xn
    h2 = _norm_mod(xn, gf_ref[...], mod_ref[3:4, :], mod_ref[4:5, :])
    h2_ref[...] = h2
    h_hi, h_lo = _split_bf16(h2)
    nt_dims = (((1,), (1,)), ((), ()))
    lt = lax.dot_general(rwh_ref[...], h_hi, nt_dims, preferred_element_type=F32)
    lt += lax.dot_general(rwh_ref[...], h_lo, nt_dims, preferred_element_type=F32)
    lt += lax.dot_general(rwl_ref[...], h_hi, nt_dims, preferred_element_type=F32)
    lt = lt + rb_ref[...]
    e1, e2, w1, w2 = _route(lt)
    @pl.when(pl.program_id(0) == 0)
    def _():
        cnt_ref[...] = jnp.zeros_like(cnt_ref)

    eid = lax.broadcasted_iota(jnp.int32, (N_EXPERTS, TM), 0).astype(F32)
    hit1 = eid == e1
    hit2 = eid == e2
    onehot = jnp.where(hit1 | hit2, 1.0, 0.0)
    before = lax.broadcasted_iota(jnp.int32, (TM, TM), 0) < lax.broadcasted_iota(jnp.int32, (TM, TM), 1)
    excl = jnp.dot(onehot.astype(BF16), jnp.where(before, 1.0, 0.0).astype(BF16),
                   preferred_element_type=F32)
    seen = excl + cnt_ref[:, 0:1]
    r1 = jnp.sum(jnp.where(hit1, seen, 0.0), axis=0, keepdims=True)
    r2 = jnp.sum(jnp.where(hit2, seen, 0.0), axis=0, keepdims=True)
    cnt_ref[...] = cnt_ref[...] + jnp.sum(onehot, axis=1, keepdims=True)
    c_ref[...] = cnt_ref[...]
    zero = jnp.zeros_like(e1)
    r_ref[...] = jnp.concatenate([e1, e2, w1, w2, r1, r2, zero, zero], axis=0)


def _merge(xc, u, yf, yb, bp, gates, mod, d_skip, pmt, wa, wb, wo, gf, rwh, rwl, rb, n_ctx_tiles):
    n = xc.shape[0]
    nt = n // TM
    kind = lambda i: (jnp.where(i < n_ctx_tiles, 0, 1), 0, 0)
    tile = lambda w: pl.BlockSpec((TM, w), lambda i: (i, 0))
    full = lambda a: pl.BlockSpec(a.shape, lambda i: (0,) * a.ndim)
    return pl.pallas_call(
        _merge_kernel,
        grid=(nt,),
        in_specs=[
            tile(D_MODEL), tile(SSM_WIDTH), tile(SSM_WIDTH), tile(SSM_WIDTH), tile(CONV_WIDTH),
            tile(2 * D_MODEL),
            pl.BlockSpec((None, N_MOD, D_MODEL), kind),
            full(d_skip), full(pmt), full(wa), full(wb), full(wo), full(gf), full(rwh), full(rwl),
            full(rb),
        ],
        out_specs=[
            tile(D_MODEL), tile(D_MODEL),
            pl.BlockSpec((SUBLANES, TM), lambda i: (0, i)),
            pl.BlockSpec((N_EXPERTS, LANES), lambda i: (0, 0)),
        ],
        out_shape=[
            jax.ShapeDtypeStruct((n, D_MODEL), F32),
            jax.ShapeDtypeStruct((n, D_MODEL), F32),
            jax.ShapeDtypeStruct((SUBLANES, n), F32),
            jax.ShapeDtypeStruct((N_EXPERTS, LANES), F32),
        ],
        scratch_shapes=[pltpu.VMEM((N_EXPERTS, LANES), F32)],
        compiler_params=_cparams(("arbitrary",)),
        name="merge",
    )(xc, u, yf, yb, bp, gates, mod, d_skip, pmt, wa, wb, wo, gf, rwh, rwl, rb)


TMM = 256


def _row_copy(src, i, dst, j, sem):
    return pltpu.make_async_copy(src.at[pl.ds(i, 1)], dst.at[pl.ds(j, 1)], sem)


def _dispatch_kernel(s1_ref, s2_ref, zf_ref, h_hbm, zeros_hbm, xs_hbm, sem, zsem):
    i = pl.program_id(0)
    nt = pl.num_programs(0)

    @pl.when(i == 0)
    def _():
        def fill(t, start):
            @pl.when(zf_ref[t] != 0)
            def _():
                row0 = pl.multiple_of(t * TMM, TMM) if start else 0
                cp = pltpu.make_async_copy(zeros_hbm, xs_hbm.at[pl.ds(row0, TMM)], zsem)
                cp.start() if start else cp.wait()

        lax.fori_loop(0, zf_ref.shape[0], lambda t, c: (fill(t, True), c)[1], 0)
        lax.fori_loop(0, zf_ref.shape[0], lambda t, c: (fill(t, False), c)[1], 0)

    def issue(t, carry):
        tok = i * TM + t
        _row_copy(h_hbm, tok, xs_hbm, s1_ref[tok], sem.at[i % 2]).start()
        _row_copy(h_hbm, tok, xs_hbm, s2_ref[tok], sem.at[i % 2]).start()
        return carry

    lax.fori_loop(0, TM, issue, 0, unroll=8)

    def drain(slot):
        def body(t, carry):
            _row_copy(h_hbm, 0, xs_hbm, 0, sem.at[slot]).wait()
            _row_copy(h_hbm, 0, xs_hbm, 0, sem.at[slot]).wait()
            return carry

        lax.fori_loop(0, TM, body, 0, unroll=8)

    @pl.when(i > 0)
    def _():
        drain((i + 1) % 2)

    @pl.when(i == nt - 1)
    def _():
        drain(i % 2)


def _dispatch(h2, s1, s2, zero_fill, ns):
    n = h2.shape[0]
    return pl.pallas_call(
        _dispatch_kernel,
        grid_spec=pltpu.PrefetchScalarGridSpec(
            num_scalar_prefetch=3,
            grid=(n // TM,),
            in_specs=[pl.BlockSpec(memory_space=pl.ANY), pl.BlockSpec(memory_space=pl.ANY)],
            out_specs=pl.BlockSpec(memory_space=pl.ANY),
            scratch_shapes=[pltpu.SemaphoreType.DMA((2,)), pltpu.SemaphoreType.DMA(())],
        ),
        out_shape=jax.ShapeDtypeStruct((ns, D_MODEL), F32),
        compiler_params=_cparams(("arbitrary",)),
        name="moe_dispatch",
    )(s1, s2, zero_fill, h2, jnp.zeros((TMM, D_MODEL), F32))


def _ffn_kernel(te_ref, nu_ref, x_ref, wg_ref, wu_ref, wd_ref, y_ref):
    @pl.when(pl.program_id(0) < nu_ref[0])
    def _():
        x = x_ref[...].astype(BF16)
        hid = jnp.dot(x, wg_ref[...], preferred_element_type=F32)
        hid = hid * jax.nn.sigmoid(hid) * jnp.dot(x, wu_ref[...], preferred_element_type=F32)
        y_ref[...] = jnp.dot(hid.astype(BF16), wd_ref[...], preferred_element_type=F32)

    @pl.when(pl.program_id(0) >= nu_ref[0])
    def _():
        y_ref[...] = jnp.zeros_like(y_ref)


def _ffn(xs, tile_expert, n_used, wg, wu, wd):
    ns = xs.shape[0]
    return pl.pallas_call(
        _ffn_kernel,
        grid_spec=pltpu.PrefetchScalarGridSpec(
            num_scalar_prefetch=2,
            grid=(ns // TMM,),
            in_specs=[
                pl.BlockSpec((TMM, D_MODEL), lambda i, te, nu: (i, 0)),
                pl.BlockSpec((None, D_MODEL, D_EXPERT), lambda i, te, nu: (te[i], 0, 0)),
                pl.BlockSpec((None, D_MODEL, D_EXPERT), lambda i, te, nu: (te[i], 0, 0)),
                pl.BlockSpec((None, D_EXPERT, D_MODEL), lambda i, te, nu: (te[i], 0, 0)),
            ],
            out_specs=pl.BlockSpec((TMM, D_MODEL), lambda i, te, nu: (i, 0)),
        ),
        out_shape=jax.ShapeDtypeStruct((ns, D_MODEL), F32),
        compiler_params=_cparams(("arbitrary",)),
        name="moe_ffn",
    )(tile_expert, n_used, xs, wg, wu, wd)


def _combine_kernel(s1_ref, s2_ref, x_ref, rt_ref, mod_ref, ys_hbm, o_ref, buf, sem):
    i = pl.program_id(0)
    nt = pl.num_programs(0)

    def issue(tile, slot):
        def body(r, carry):
            t = tile * TM + r
            _row_copy(ys_hbm, s1_ref[t], buf.at[slot], r, sem.at[slot]).start()
            _row_copy(ys_hbm, s2_ref[t], buf.at[slot], TM + r, sem.at[slot]).start()
            return carry

        lax.fori_loop(0, TM, body, 0, unroll=8)

    @pl.when(i == 0)
    def _():
        issue(0, 0)

    @pl.when(i + 1 < nt)
    def _():
        issue(i + 1, (i + 1) % 2)

    slot = i % 2

    def drain(r, carry):
        _row_copy(ys_hbm, 0, buf.at[slot], 0, sem.at[slot]).wait()
        _row_copy(ys_hbm, 0, buf.at[slot], 0, sem.at[slot]).wait()
        return carry

    lax.fori_loop(0, TM, drain, 0, unroll=8)
    rt = rt_ref[...]
    y = rt[:, 2:3] * buf[slot, 0:TM, :] + rt[:, 3:4] * buf[slot, TM:2 * TM, :]
    o_ref[...] = x_ref[...] + mod_ref[5:6, :] * y


def _combine(xn, rt, mod, ys, s1, s2, n_ctx_tiles):
    n = xn.shape[0]
    kind = lambda i, a, b: (jnp.where(i < n_ctx_tiles, 0, 1), 0, 0)
    return pl.pallas_call(
        _combine_kernel,
        grid_spec=pltpu.PrefetchScalarGridSpec(
            num_scalar_prefetch=2,
            grid=(n // TM,),
            in_specs=[
                pl.BlockSpec((TM, D_MODEL), lambda i, a, b: (i, 0)),
                pl.BlockSpec((TM, SUBLANES), lambda i, a, b: (i, 0)),
                pl.BlockSpec((None, N_MOD, D_MODEL), kind),
                pl.BlockSpec(memory_space=pl.ANY),
            ],
            out_specs=pl.BlockSpec((TM, D_MODEL), lambda i, a, b: (i, 0)),
            scratch_shapes=[
                pltpu.VMEM((2, 2 * TM, D_MODEL), F32),
                pltpu.SemaphoreType.DMA((2,)),
            ],
        ),
        out_shape=jax.ShapeDtypeStruct((n, D_MODEL), F32),
        compiler_params=_cparams(("arbitrary",)),
        name="moe_combine",
    )(s1, s2, xn, rt, mod, ys)


def _moe(xn, h2, r, counts, mod, wg, wu, wd, n_ctx_tiles):
    n = xn.shape[0]
    ns = 2 * n + N_EXPERTS * TMM
    cnt = counts[:, 0].astype(jnp.int32)
    padded = (cnt + TMM - 1) // TMM * TMM
    ends = jnp.cumsum(padded)
    off = ends - padded
    e1 = r[0].astype(jnp.int32)
    e2 = r[1].astype(jnp.int32)
    s1 = jnp.take(off, e1) + r[4].astype(jnp.int32)
    s2 = jnp.take(off, e2) + r[5].astype(jnp.int32)
    tile_start = jnp.arange(ns // TMM, dtype=jnp.int32) * TMM
    tile_expert = jnp.minimum(jnp.searchsorted(ends, tile_start, side='right'), N_EXPERTS - 1).astype(jnp.int32)
    n_used = (ends[-1:] // TMM).astype(jnp.int32)
    last_tile = jnp.where(padded > 0, ends // TMM - 1, -1)
    tile_id = jnp.arange(ns // TMM, dtype=jnp.int32)
    zero_fill = ((tile_id[:, None] == last_tile[None, :]).any(axis=1) | (tile_id >= n_used[0])).astype(jnp.int32)
    xs = _dispatch(h2, s1, s2, zero_fill, ns)
    ys = _ffn(xs, tile_expert, n_used, wg, wu, wd)
    return _combine(xn, r.T, mod, ys, s1, s2, n_ctx_tiles)


def _final_kernel(x_ref, g_ref, o_ref):
    x = x_ref[...]
    o_ref[...] = x * lax.rsqrt(jnp.mean(x * x, axis=-1, keepdims=True) + NORM_EPS) * g_ref[...]


def _final(xc, g, n_ctx_tiles):
    n = xc.shape[0] - n_ctx_tiles * TM
    return pl.pallas_call(
        _final_kernel,
        grid=(n // TM,),
        in_specs=[
            pl.BlockSpec((TM, D_MODEL), lambda i: (i + n_ctx_tiles, 0)),
            pl.BlockSpec((1, D_MODEL), lambda i: (0, 0)),
        ],
        out_specs=pl.BlockSpec((TM, D_MODEL), lambda i: (i, 0)),
        out_shape=jax.ShapeDtypeStruct((n, D_MODEL), F32),
        compiler_params=_cparams(("arbitrary",)),
        name="final_norm",
    )(xc, g)


def kernel(x, c, ctx, c_ctx, ada_w, ada_b, norm_mix_g, norm_ffn_g, w_in, ssm_a_re, ssm_a_im, ssm_log_dt, ssm_b_re, ssm_b_im, ssm_c_re, ssm_c_im, ssm_d, w_a_val, w_a_gate, conv_w, conv_b, conv_ln_g, conv_ln_b, w_b_out, w_out, router_w, router_b, w_exp_gate, w_exp_up, w_exp_down, final_norm_g):
    depth = ada_w.shape[0]
    bsz, seq, _ = x.shape
    n_ctx = ctx.shape[1]
    assert bsz == 1 and n_ctx == TM and seq % TM == 0 and seq % GRID_W == 0
    n_ctx_tiles = n_ctx // TM
    n = n_ctx + seq

    xc = jnp.concatenate([ctx[0], x[0]], axis=0)
    cc = jnp.zeros((SUBLANES, D_MODEL), F32).at[0].set(c_ctx).at[1].set(c[0])
    mods = _mods(cc, ada_w, ada_b)[:, :2].reshape(depth, 2, N_MOD, D_MODEL)

    rwt = router_w.T
    rwh, rwl = _split_bf16(rwt)
    rb = router_b.reshape(N_EXPERTS, 1)
    pm, pmt = _ssm_perm()

    for l in range(depth):
        last = l == depth - 1
        u, z, gates = _pre(xc, mods[l], norm_mix_g[l].reshape(1, -1), w_in[l].astype(BF16), pm,
                           n_ctx_tiles)
        prm = [_ssm_params(ssm_a_re[l, k], ssm_a_im[l, k], ssm_log_dt[l, k], ssm_b_re[l, k],
                           ssm_b_im[l, k], ssm_c_re[l, k], ssm_c_im[l, k]) for k in range(2)]
        ys = _ssm(u, *[jnp.stack(pair) for pair in zip(*prm)], n_ctx_tiles)
        vec = lambda a: a.reshape(1, -1)
        conv_args = (conv_w[l], vec(conv_b[l]), vec(conv_ln_g[l]), vec(conv_ln_b[l]))
        if l % 2 == 0:
            bp = _conv_rows(z, *conv_args, n_ctx_tiles)
        else:
            assert last, "column conv over the context prefix is not needed by the last layer only"
            bp = _conv_cols(z, *conv_args, n_ctx)
        wa = jnp.concatenate([w_a_val[l], w_a_gate[l]], axis=1).astype(BF16)
        xn, h2, r, counts = _merge(xc, u, ys[0], ys[1], bp, gates, mods[l], vec(ssm_d[l]), pmt, wa,
                                   w_b_out[l].astype(BF16), w_out[l].astype(BF16), vec(norm_ffn_g[l]),
                                   rwh, rwl, rb, n_ctx_tiles)
        xc = _moe(xn, h2, r, counts, mods[l], w_exp_gate[l].astype(BF16), w_exp_up[l].astype(BF16),
                  w_exp_down[l].astype(BF16), n_ctx_tiles)
    return _final(xc, vec(final_norm_g), n_ctx_tiles)[None]
```

```python
import functools
import math

import jax
import jax.numpy as jnp
from jax import lax
from jax.experimental import pallas as pl
from jax.experimental.pallas import tpu as pltpu

F32 = jnp.float32
BF16 = jnp.bfloat16

D_MODEL = 1024
GRID_W = 64
SSM_WIDTH = 512
SSM_GROUP = 16
SSM_GROUPS = SSM_WIDTH // SSM_GROUP
SSM_STATE = 64
CONV_WIDTH = 512
CONV_SIZE = 31
CONV_HALF = CONV_SIZE // 2
IN_COLS = SSM_WIDTH + 2 * CONV_WIDTH + 2 * D_MODEL
N_EXPERTS = 16
N_EXPERT_GROUPS = 4
EXPERTS_PER_GROUP = N_EXPERTS // N_EXPERT_GROUPS
D_EXPERT = 512
N_MOD = 6
NORM_EPS = 1e-6

TM = 256
LANES = 128
SUBLANES = 8
SEG = TM // SUBLANES
KB = SSM_WIDTH // LANES
GPB = LANES // SSM_GROUP
SB = GPB * SSM_STATE
PAD = 16
CHUNK = 32
VMEM_LIMIT = 56 * 1024 * 1024


def _cparams(sem):
    return pltpu.CompilerParams(dimension_semantics=sem, vmem_limit_bytes=VMEM_LIMIT)


def _split_bf16(a):
    hi = a.astype(BF16)
    lo = (a - hi.astype(F32)).astype(BF16)
    return hi, lo


def _mods_kernel(c_ref, w_ref, b_ref, o_ref):
    s = c_ref[...]
    s = s * jax.nn.sigmoid(s)
    s_hi, s_lo = _split_bf16(s)
    w_hi, w_lo = _split_bf16(w_ref[...])
    acc = jnp.dot(s_hi, w_hi, preferred_element_type=F32)
    acc += jnp.dot(s_lo, w_hi, preferred_element_type=F32)
    acc += jnp.dot(s_hi, w_lo, preferred_element_type=F32)
    o_ref[...] = acc + b_ref[...]


def _mods(cc, ada_w, ada_b):
    depth = ada_w.shape[0]
    cols = ada_w.shape[2]
    cb = 1536
    return pl.pallas_call(
        _mods_kernel,
        grid=(depth, cols // cb),
        in_specs=[
            pl.BlockSpec((SUBLANES, D_MODEL), lambda l, j: (0, 0)),
            pl.BlockSpec((None, D_MODEL, cb), lambda l, j: (l, 0, j)),
            pl.BlockSpec((None, 1, cb), lambda l, j: (l, 0, j)),
        ],
        out_specs=pl.BlockSpec((None, SUBLANES, cb), lambda l, j: (l, 0, j)),
        out_shape=jax.ShapeDtypeStruct((depth, SUBLANES, cols), F32),
        compiler_params=_cparams(("arbitrary", "arbitrary")),
        name="mods",
    )(cc, ada_w, ada_b.reshape(depth, 1, cols))


def _norm_mod(x, g, shift, scale):
    y = x * lax.rsqrt(jnp.mean(x * x, axis=-1, keepdims=True) + NORM_EPS) * g
    return y * (1.0 + scale) + shift


def _pre_kernel(x_ref, mod_ref, g_ref, w_ref, pm_ref, u_ref, z_ref, gate_ref):
    h = _norm_mod(x_ref[...], g_ref[...], mod_ref[0:1, :], mod_ref[1:2, :])
    p = jnp.dot(h.astype(BF16), w_ref[...], preferred_element_type=F32)
    u_ref[...] = jnp.dot(pm_ref[...], p[:, :SSM_WIDTH].astype(BF16),
                         preferred_element_type=F32).astype(BF16)
    v1 = p[:, SSM_WIDTH:SSM_WIDTH + CONV_WIDTH]
    v2 = p[:, SSM_WIDTH + CONV_WIDTH:SSM_WIDTH + 2 * CONV_WIDTH]
    z_ref[...] = v1 * jax.nn.sigmoid(v2)
    gate_ref[...] = jax.nn.sigmoid(p[:, SSM_WIDTH + 2 * CONV_WIDTH:]).astype(BF16)


def _pre(xc, mod, g, w_in_bf, pm, n_ctx_tiles):
    n = xc.shape[0]
    nt = n // TM
    kind = lambda i: (jnp.where(i < n_ctx_tiles, 0, 1), 0, 0)
    return pl.pallas_call(
        _pre_kernel,
        grid=(nt,),
        in_specs=[
            pl.BlockSpec((TM, D_MODEL), lambda i: (i, 0)),
            pl.BlockSpec((None, N_MOD, D_MODEL), kind),
            pl.BlockSpec((1, D_MODEL), lambda i: (0, 0)),
            pl.BlockSpec((D_MODEL, IN_COLS), lambda i: (0, 0)),
            pl.BlockSpec((TM, TM), lambda i: (0, 0)),
        ],
        out_specs=[
            pl.BlockSpec((TM, SSM_WIDTH), lambda i: (i, 0)),
            pl.BlockSpec((TM, CONV_WIDTH), lambda i: (i, 0)),
            pl.BlockSpec((TM, 2 * D_MODEL), lambda i: (i, 0)),
        ],
        out_shape=[
            jax.ShapeDtypeStruct((n, SSM_WIDTH), BF16),
            jax.ShapeDtypeStruct((n, CONV_WIDTH), F32),
            jax.ShapeDtypeStruct((n, 2 * D_MODEL), BF16),
        ],
        compiler_params=_cparams(("arbitrary",)),
        name="pre",
    )(xc, mod, g, w_in_bf, pm)


def _ssm_kernel(uf_ref, ub_ref, bw_ref, cw_ref, ar_ref, ai_ref, qr_ref, qi_ref,
                yf_ref, yb_ref, cr_ref, ci_ref, sf_ref, sb_ref, hf_ref, hb_ref):
    @pl.when(pl.program_id(1) == 0)
    def _():
        cr_ref[...] = jnp.zeros_like(cr_ref)
        ci_ref[...] = jnp.zeros_like(ci_ref)

    dirs = (0, 1)
    u_refs = (uf_ref, ub_ref)
    y_refs = (yf_ref, yb_ref)
    s_refs = (sf_ref, sb_ref)
    for d in dirs:
        s_refs[d][...] = jnp.dot(u_refs[d][...], bw_ref[d], preferred_element_type=F32)

    ar = [ar_ref[d] for d in dirs]
    ai = [ai_ref[d] for d in dirs]

    def group(d, i):
        return SEG - 1 - i if d == 1 else i

    def advance(carry, i):
        out = []
        for d in dirs:
            rows = pl.ds(pl.multiple_of(group(d, i) * SUBLANES, SUBLANES), SUBLANES)
            hr_d, hi_d = carry[2 * d], carry[2 * d + 1]
            out.append(ar[d] * hr_d - ai[d] * hi_d + s_refs[d][rows, 0:SB])
            out.append(ar[d] * hi_d + ai[d] * hr_d + s_refs[d][rows, SB:2 * SB])
        return tuple(out)

    zero = jnp.zeros((SUBLANES, SB), F32)
    ends = lax.fori_loop(0, SEG, lambda i, c: advance(c, i), (zero,) * 4, unroll=2)
    hr = [ends[0], ends[2]]
    hi = [ends[1], ends[3]]

    hpr = [None, None]
    hpi = [None, None]
    for d in dirs:
        asr = qr_ref[d, 0:1, :]
        asi = qi_ref[d, 0:1, :]
        pr_ = cr_ref[d]
        pi_ = ci_ref[d]
        prev_r = [None] * SUBLANES
        prev_i = [None] * SUBLANES
        order = range(SUBLANES - 1, -1, -1) if d == 1 else range(SUBLANES)
        for s in order:
            prev_r[s] = pr_
            prev_i[s] = pi_
            er = hr[d][s:s + 1, :]
            ei = hi[d][s:s + 1, :]
            pr_, pi_ = er + asr * pr_ - asi * pi_, ei + asr * pi_ + asi * pr_
        cr_ref[d] = pr_
        ci_ref[d] = pi_
        hpr[d] = jnp.concatenate(prev_r, axis=0)
        hpi[d] = jnp.concatenate(prev_i, axis=0)

    h_refs = (hf_ref, hb_ref)

    def rescan(j, carry):
        first = advance(carry, 2 * j)
        second = advance(first, 2 * j + 1)
        for d in dirs:
            lo, hi_ = (first, second) if d == 0 else (second, first)
            pair = j if d == 0 else SEG // 2 - 1 - j
            rows = pl.ds(pl.multiple_of(pair * 2 * SUBLANES, 2 * SUBLANES), 2 * SUBLANES)
            h_refs[d][rows, 0:SB] = jnp.concatenate([lo[2 * d], hi_[2 * d]], axis=0).astype(BF16)
            h_refs[d][rows, SB:2 * SB] = jnp.concatenate(
                [lo[2 * d + 1], hi_[2 * d + 1]], axis=0).astype(BF16)
        return second

    lax.fori_loop(0, SEG // 2, rescan, (hpr[0], hpi[0], hpr[1], hpi[1]))

    for d in dirs:
        y_refs[d][...] = jnp.dot(h_refs[d][...], cw_ref[d], preferred_element_type=F32)


def _ssm(u, bw, cw, ar, ai, qr, qi, n_ctx_tiles):
    n = u.shape[0]
    nt = n // TM

    def fwd(k, t):
        return (t, k)

    def bwd(k, t):
        return (jnp.where(t < n_ctx_tiles, n_ctx_tiles - 1 - t, nt - 1 - (t - n_ctx_tiles)), k)

    per_k = lambda *shape: pl.BlockSpec((2, None) + shape, lambda k, t: (0, k) + (0,) * len(shape))
    return pl.pallas_call(
        _ssm_kernel,
        grid=(KB, nt),
        in_specs=[
            pl.BlockSpec((TM, LANES), fwd),
            pl.BlockSpec((TM, LANES), bwd),
            per_k(LANES, 2 * SB),
            per_k(2 * SB, LANES),
            per_k(SUBLANES, SB),
            per_k(SUBLANES, SB),
            per_k(SUBLANES, SB),
            per_k(SUBLANES, SB),
        ],
        out_specs=[pl.BlockSpec((TM, LANES), fwd), pl.BlockSpec((TM, LANES), bwd)],
        out_shape=[jax.ShapeDtypeStruct((n, SSM_WIDTH), F32)] * 2,
        scratch_shapes=[
            pltpu.VMEM((2, 1, SB), F32),
            pltpu.VMEM((2, 1, SB), F32),
            pltpu.VMEM((TM, 2 * SB), F32),
            pltpu.VMEM((TM, 2 * SB), F32),
            pltpu.VMEM((TM, 2 * SB), BF16),
            pltpu.VMEM((TM, 2 * SB), BF16),
        ],
        compiler_params=_cparams(("arbitrary", "arbitrary")),
        name="ssm",
    )(u, u, bw, cw, ar, ai, qr, qi)


def _ssm_perm():
    p = jnp.arange(TM)
    src = SEG * (p % SUBLANES) + p // SUBLANES
    pm = (src[:, None] == jnp.arange(TM)[None, :]).astype(BF16)
    return pm, pm.T


def _ssm_params(a_re, a_im, log_dt, b_re, b_im, c_re, c_im):
    lam_re = jnp.minimum(a_re.astype(F32), -1e-4)
    lam_im = a_im.astype(F32)
    dt = jnp.exp(log_dt.astype(F32))[:, None]
    mag = jnp.exp(lam_re * dt)
    abar_re = mag * jnp.cos(lam_im * dt)
    abar_im = mag * jnp.sin(lam_im * dt)
    n_re, n_im = abar_re - 1.0, abar_im
    den = lam_re * lam_re + lam_im * lam_im
    coef_re = (n_re * lam_re + n_im * lam_im) / den
    coef_im = (n_im * lam_re - n_re * lam_im) / den
    br, bi = b_re.astype(F32), b_im.astype(F32)
    bbar_re = coef_re[..., None] * br - coef_im[..., None] * bi
    bbar_im = coef_re[..., None] * bi + coef_im[..., None] * br
    eye = jnp.eye(GPB, dtype=F32)

    def in_block(b):
        b = b.reshape(KB, GPB, SSM_STATE, SSM_GROUP)
        return jnp.einsum('kgpc,gh->kgchp', b, eye).reshape(KB, LANES, SB)

    def out_block(c):
        c = c.reshape(KB, GPB, SSM_GROUP, SSM_STATE)
        return jnp.einsum('kgcp,gh->kgphc', c, eye).reshape(KB, SB, LANES)

    bw = jnp.concatenate([in_block(bbar_re), in_block(bbar_im)], axis=2).astype(BF16)
    cw = jnp.concatenate([out_block(c_re.astype(F32)), -out_block(c_im.astype(F32))], axis=1).astype(BF16)
    qr, qi = abar_re, abar_im
    for _ in range(SEG - 1):
        qr, qi = qr * abar_re - qi * abar_im, qr * abar_im + qi * abar_re
    rep = lambda a: jnp.broadcast_to(a.reshape(KB, 1, SB), (KB, SUBLANES, SB))
    return bw, cw, rep(abar_re), rep(abar_im), rep(qr), rep(qi)


def _ln_silu(acc, g, b):
    mu = jnp.mean(acc, axis=-1, keepdims=True)
    xc = acc - mu
    y = xc * lax.rsqrt(jnp.mean(xc * xc, axis=-1, keepdims=True) + NORM_EPS) * g + b
    return y * jax.nn.sigmoid(y)


def _conv_rows_kernel(z_ref, w_ref, cb_ref, g_ref, b_ref, o_ref, zp_ref, *, n_ctx_tiles):
    i = pl.program_id(0)
    w = w_ref[...]
    cb = cb_ref[...]
    g = g_ref[...]
    b = b_ref[...]

    def run(seq):
        nseq = TM // seq
        stride = seq + 2 * PAD
        total = nseq * stride
        zeros = jnp.zeros((PAD, CONV_WIDTH), F32)
        for a in range(nseq):
            zp_ref[0, stride * a:stride * a + PAD, :] = zeros
            zp_ref[0, stride * a + PAD:stride * a + PAD + seq, :] = z_ref[seq * a:seq * (a + 1), :]
            zp_ref[0, stride * a + PAD + seq:stride * (a + 1), :] = zeros
        for k in range(1, SUBLANES):
            for r0 in range(0, total - SUBLANES, CHUNK):
                rows = min(CHUNK, total - SUBLANES - r0)
                zp_ref[k, r0:r0 + rows, :] = zp_ref[0, r0 + k:r0 + k + rows, :]
        for a in range(nseq):
            for r0 in range(0, seq, CHUNK):
                acc = jnp.zeros((CHUNK, CONV_WIDTH), F32) + cb
                for d in range(CONV_SIZE):
                    off = PAD + d - CONV_HALF
                    start = stride * a + r0 + off // SUBLANES * SUBLANES
                    acc = acc + w[d:d + 1, :] * zp_ref[off % SUBLANES, start:start + CHUNK, :]
                o_ref[seq * a + r0:seq * a + r0 + CHUNK, :] = _ln_silu(acc, g, b).astype(o_ref.dtype)

    @pl.when(i < n_ctx_tiles)
    def _():
        run(TM)

    @pl.when(i >= n_ctx_tiles)
    def _():
        run(GRID_W)


def _conv_rows(z, w, cb, g, b, n_ctx_tiles):
    n = z.shape[0]
    nt = n // TM
    vec = pl.BlockSpec((1, CONV_WIDTH), lambda i: (0, 0))
    return pl.pallas_call(
        functools.partial(_conv_rows_kernel, n_ctx_tiles=n_ctx_tiles),
        grid=(nt,),
        in_specs=[
            pl.BlockSpec((TM, CONV_WIDTH), lambda i: (i, 0)),
            pl.BlockSpec((CONV_SIZE, CONV_WIDTH), lambda i: (0, 0)),
            vec, vec, vec,
        ],
        out_specs=pl.BlockSpec((TM, CONV_WIDTH), lambda i: (i, 0)),
        out_shape=jax.ShapeDtypeStruct((n, CONV_WIDTH), BF16),
        scratch_shapes=[pltpu.VMEM((SUBLANES, TM // GRID_W * (GRID_W + 2 * PAD), CONV_WIDTH), F32)],
        compiler_params=_cparams(("arbitrary",)),
        name="conv_rows",
    )(z, w, cb, g, b)


WB = 16
RB = 16


def _conv_cols_kernel(z_ref, w_ref, cb_ref, g_ref, b_ref, o_ref, zp_ref, *, r_ctx, rows):
    w = w_ref[...]
    cb = cb_ref[...]
    g = g_ref[...]
    b = b_ref[...]
    zeros = jnp.zeros((CONV_HALF, WB, CONV_WIDTH), F32)
    zp_ref[0:CONV_HALF] = zeros
    zp_ref[CONV_HALF + rows:CONV_HALF + rows + CONV_HALF] = zeros
    zp_ref[CONV_HALF:CONV_HALF + rows] = z_ref[r_ctx:r_ctx + rows]
    o_ref[0:r_ctx] = jnp.zeros((r_ctx, WB, CONV_WIDTH), o_ref.dtype)

    def column(h, c):
        hs = slice(SUBLANES * h, SUBLANES * (h + 1))
        ls = slice(LANES * c, LANES * (c + 1))
        wv = [jnp.broadcast_to(w[d:d + 1, ls], (SUBLANES, LANES)) for d in range(CONV_SIZE)]
        bias = jnp.broadcast_to(cb[:, ls], (SUBLANES, LANES))

        def body(blk, carry):
            r0 = blk * RB
            acc = [bias] * RB
            for q in range(RB + CONV_SIZE - 1):
                zq = zp_ref[r0 + q, hs, ls]
                for j in range(RB):
                    d = q - j
                    if 0 <= d < CONV_SIZE:
                        acc[j] = acc[j] + wv[d] * zq
            for j in range(RB):
                zp_ref[r0 + j, hs, ls] = acc[j]
            return carry

        lax.fori_loop(0, rows // RB, body, 0)

    for h in range(WB // SUBLANES):
        for c in range(CONV_WIDTH // LANES):
            column(h, c)

    def norm(blk, carry):
        r0 = blk * SUBLANES
        v = zp_ref[pl.ds(r0, SUBLANES)].reshape(SUBLANES * WB, CONV_WIDTH)
        y = _ln_silu(v, g, b).astype(o_ref.dtype)
        o_ref[pl.ds(r_ctx + r0, SUBLANES)] = y.reshape(SUBLANES, WB, CONV_WIDTH)
        return carry

    lax.fori_loop(0, rows // SUBLANES, norm, 0)


def _conv_cols(z, w, cb, g, b, n_ctx):
    n = z.shape[0]
    r_all = n // GRID_W
    r_ctx = n_ctx // GRID_W
    rows = r_all - r_ctx
    z3 = z.reshape(r_all, GRID_W, CONV_WIDTH)
    vec = pl.BlockSpec((1, CONV_WIDTH), lambda j: (0, 0))
    out = pl.pallas_call(
        functools.partial(_conv_cols_kernel, r_ctx=r_ctx, rows=rows),
        grid=(GRID_W // WB,),
        in_specs=[
            pl.BlockSpec((r_all, WB, CONV_WIDTH), lambda j: (0, j, 0)),
            pl.BlockSpec((CONV_SIZE, CONV_WIDTH), lambda j: (0, 0)),
            vec, vec, vec,
        ],
        out_specs=pl.BlockSpec((r_all, WB, CONV_WIDTH), lambda j: (0, j, 0)),
        out_shape=jax.ShapeDtypeStruct((r_all, GRID_W, CONV_WIDTH), BF16),
        scratch_shapes=[pltpu.VMEM((rows + 2 * CONV_HALF, WB, CONV_WIDTH), F32)],
        compiler_params=_cparams(("arbitrary",)),
        name="conv_cols",
    )(z3, w, cb, g, b)
    return out.reshape(n, CONV_WIDTH)


def _route(lt):
    l = [lt[e:e + 1, :] for e in range(N_EXPERTS)]
    m = l[0]
    for e in range(1, N_EXPERTS):
        m = jnp.maximum(m, l[e])
    ex = [jnp.exp(v - m) for v in l]
    den = ex[0]
    for e in range(1, N_EXPERTS):
        den = den + ex[e]
    p = [v / den for v in ex]
    scores = []
    for gidx in range(N_EXPERT_GROUPS):
        q = p[gidx * EXPERTS_PER_GROUP:(gidx + 1) * EXPERTS_PER_GROUP]
        best = None
        for a in range(EXPERTS_PER_GROUP):
            for c in range(a + 1, EXPERTS_PER_GROUP):
                pair = q[a] + q[c]
                best = pair if best is None else jnp.maximum(best, pair)
        scores.append(best)
    bs = scores[0]
    bg = jnp.zeros_like(bs)
    for gidx in range(1, N_EXPERT_GROUPS):
        better = scores[gidx] > bs
        bs = jnp.where(better, scores[gidx], bs)
        bg = jnp.where(better, float(gidx), bg)
    q = []
    for a in range(EXPERTS_PER_GROUP):
        v = p[a]
        for gidx in range(1, N_EXPERT_GROUPS):
            v = jnp.where(bg == float(gidx), p[gidx * EXPERTS_PER_GROUP + a], v)
        q.append(v)
    p1 = q[0]
    i1 = jnp.zeros_like(p1)
    for a in range(1, EXPERTS_PER_GROUP):
        better = q[a] > p1
        p1 = jnp.where(better, q[a], p1)
        i1 = jnp.where(better, float(a), i1)
    p2 = jnp.full_like(p1, -1.0)
    i2 = jnp.zeros_like(p1)
    for a in range(EXPERTS_PER_GROUP):
        better = (q[a] > p2) & (i1 != float(a))
        p2 = jnp.where(better, q[a], p2)
        i2 = jnp.where(better, float(a), i2)
    tot = p1 + p2
    base = bg * float(EXPERTS_PER_GROUP)
    return base + i1, base + i2, p1 / tot, p2 / tot


def _merge_kernel(x_ref, u_ref, yf_ref, yb_ref, bp_ref, gate_ref, mod_ref, d_ref, pmt_ref, wa_ref, wb_ref,
                  wo_ref, gf_ref, rwh_ref, rwl_ref, rb_ref, xo_ref, h2_ref, r_ref, c_ref, cnt_ref):
    ya = jax.nn.gelu(yf_ref[...] + yb_ref[...] + d_ref[...] * u_ref[...].astype(F32), approximate=True)
    ya = jnp.dot(pmt_ref[...], ya.astype(BF16), preferred_element_type=F32).astype(BF16)
    av = jnp.dot(ya, wa_ref[...], preferred_element_type=F32)
    a_out = av[:, :D_MODEL] * jax.nn.sigmoid(av[:, D_MODEL:])
    b_out = jnp.dot(bp_ref[...], wb_ref[...], preferred_element_type=F32)
    gate = gate_ref[...].astype(F32)
    merged = gate[:, :D_MODEL] * a_out + gate[:, D_MODEL:] * b_out
    mx = jnp.dot(merged.astype(BF16), wo_ref[...], preferred_element_type=F32)
    xn = x_ref[...] + mod_ref[2:3, :] * mx
    xo_ref[...] = xn
    h2 = _norm_mod(xn, gf_ref[...], mod_ref[3:4, :], mod_ref[4:5, :])
    h2_ref[...] = h2
    h_hi, h_lo = _split_bf16(h2)
    nt_dims = (((1,), (1,)), ((), ()))
    lt = lax.dot_general(rwh_ref[...], h_hi, nt_dims, preferred_element_type=F32)
    lt += lax.dot_general(rwh_ref[...], h_lo, nt_dims, preferred_element_type=F32)
    lt += lax.dot_general(rwl_ref[...], h_hi, nt_dims, preferred_element_type=F32)
    lt = lt + rb_ref[...]
    e1, e2, w1, w2 = _route(lt)
    @pl.when(pl.program_id(0) == 0)
    def _():
        cnt_ref[...] = jnp.zeros_like(cnt_ref)

    eid = lax.broadcasted_iota(jnp.int32, (N_EXPERTS, TM), 0).astype(F32)
    hit1 = eid == e1
    hit2 = eid == e2
    onehot = jnp.where(hit1 | hit2, 1.0, 0.0)
    before = lax.broadcasted_iota(jnp.int32, (TM, TM), 0) < lax.broadcasted_iota(jnp.int32, (TM, TM), 1)
    excl = jnp.dot(onehot.astype(BF16), jnp.where(before, 1.0, 0.0).astype(BF16),
                   preferred_element_type=F32)
    seen = excl + cnt_ref[:, 0:1]
    r1 = jnp.sum(jnp.where(hit1, seen, 0.0), axis=0, keepdims=True)
    r2 = jnp.sum(jnp.where(hit2, seen, 0.0), axis=0, keepdims=True)
    cnt_ref[...] = cnt_ref[...] + jnp.sum(onehot, axis=1, keepdims=True)
    c_ref[...] = cnt_ref[...]
    zero = jnp.zeros_like(e1)
    r_ref[...] = jnp.concatenate([e1, e2, w1, w2, r1, r2, zero, zero], axis=0)


def _merge(xc, u, yf, yb, bp, gates, mod, d_skip, pmt, wa, wb, wo, gf, rwh, rwl, rb, n_ctx_tiles):
    n = xc.shape[0]
    nt = n // TM
    kind = lambda i: (jnp.where(i < n_ctx_tiles, 0, 1), 0, 0)
    tile = lambda w: pl.BlockSpec((TM, w), lambda i: (i, 0))
    full = lambda a: pl.BlockSpec(a.shape, lambda i: (0,) * a.ndim)
    return pl.pallas_call(
        _merge_kernel,
        grid=(nt,),
        in_specs=[
            tile(D_MODEL), tile(SSM_WIDTH), tile(SSM_WIDTH), tile(SSM_WIDTH), tile(CONV_WIDTH),
            tile(2 * D_MODEL),
            pl.BlockSpec((None, N_MOD, D_MODEL), kind),
            full(d_skip), full(pmt), full(wa), full(wb), full(wo), full(gf), full(rwh), full(rwl),
            full(rb),
        ],
        out_specs=[
            tile(D_MODEL), tile(D_MODEL),
            pl.BlockSpec((SUBLANES, TM), lambda i: (0, i)),
            pl.BlockSpec((N_EXPERTS, LANES), lambda i: (0, 0)),
        ],
        out_shape=[
            jax.ShapeDtypeStruct((n, D_MODEL), F32),
            jax.ShapeDtypeStruct((n, D_MODEL), F32),
            jax.ShapeDtypeStruct((SUBLANES, n), F32),
            jax.ShapeDtypeStruct((N_EXPERTS, LANES), F32),
        ],
        scratch_shapes=[pltpu.VMEM((N_EXPERTS, LANES), F32)],
        compiler_params=_cparams(("arbitrary",)),
        name="merge",
    )(xc, u, yf, yb, bp, gates, mod, d_skip, pmt, wa, wb, wo, gf, rwh, rwl, rb)


TMM = 256


def _row_copy(src, i, dst, j, sem):
    return pltpu.make_async_copy(src.at[pl.ds(i, 1)], dst.at[pl.ds(j, 1)], sem)


def _dispatch_kernel(s1_ref, s2_ref, zf_ref, h_ref, zeros_hbm, xs_hbm, buf, sem, zsem):
    i = pl.program_id(0)
    nt = pl.num_programs(0)

    @pl.when(i == 0)
    def _():
        def fill(t, start):
            @pl.when(zf_ref[t] != 0)
            def _():
                row0 = pl.multiple_of(t * TMM, TMM) if start else 0
                cp = pltpu.make_async_copy(zeros_hbm, xs_hbm.at[pl.ds(row0, TMM)], zsem)
                cp.start() if start else cp.wait()

        lax.fori_loop(0, zf_ref.shape[0], lambda t, c: (fill(t, True), c)[1], 0)
        lax.fori_loop(0, zf_ref.shape[0], lambda t, c: (fill(t, False), c)[1], 0)

    def drain(slot):
        def body(t, carry):
            _row_copy(buf.at[slot], 0, xs_hbm, 0, sem.at[slot]).wait()
            _row_copy(buf.at[slot], 0, xs_hbm, 0, sem.at[slot]).wait()
            return carry

        lax.fori_loop(0, TM, body, 0, unroll=8)

    def step(slot):
        @pl.when(i >= 2)
        def _():
            drain(slot)

        buf[slot] = h_ref[...]
        for r in range(TM):
            tok = i * TM + r
            _row_copy(buf.at[slot], r, xs_hbm, s1_ref[tok], sem.at[slot]).start()
            _row_copy(buf.at[slot], r, xs_hbm, s2_ref[tok], sem.at[slot]).start()

    for slot in range(2):
        @pl.when(i % 2 == slot)
        def _():
            step(slot)

    @pl.when(i == nt - 1)
    def _():
        drain(0)
        drain(1)


def _dispatch(h2, s1, s2, zero_fill, ns):
    n = h2.shape[0]
    assert n // TM >= 2
    return pl.pallas_call(
        _dispatch_kernel,
        grid_spec=pltpu.PrefetchScalarGridSpec(
            num_scalar_prefetch=3,
            grid=(n // TM,),
            in_specs=[pl.BlockSpec((TM, D_MODEL), lambda i, a, b, c: (i, 0)),
                      pl.BlockSpec(memory_space=pl.ANY)],
            out_specs=pl.BlockSpec(memory_space=pl.ANY),
            scratch_shapes=[pltpu.VMEM((2, TM, D_MODEL), F32), pltpu.SemaphoreType.DMA((2,)),
                            pltpu.SemaphoreType.DMA(())],
        ),
        out_shape=jax.ShapeDtypeStruct((ns, D_MODEL), F32),
        compiler_params=_cparams(("arbitrary",)),
        name="moe_dispatch",
    )(s1, s2, zero_fill, h2, jnp.zeros((TMM, D_MODEL), F32))


def _ffn_kernel(te_ref, nu_ref, x_ref, wg_ref, wu_ref, wd_ref, y_ref):
    @pl.when(pl.program_id(0) < nu_ref[0])
    def _():
        x = x_ref[...].astype(BF16)
        hid = jnp.dot(x, wg_ref[...], preferred_element_type=F32)
        hid = hid * jax.nn.sigmoid(hid) * jnp.dot(x, wu_ref[...], preferred_element_type=F32)
        y_ref[...] = jnp.dot(hid.astype(BF16), wd_ref[...], preferred_element_type=F32)

    @pl.when(pl.program_id(0) >= nu_ref[0])
    def _():
        y_ref[...] = jnp.zeros_like(y_ref)


def _ffn(xs, tile_expert, n_used, wg, wu, wd):
    ns = xs.shape[0]
    return pl.pallas_call(
        _ffn_kernel,
        grid_spec=pltpu.PrefetchScalarGridSpec(
            num_scalar_prefetch=2,
            grid=(ns // TMM,),
            in_specs=[
                pl.BlockSpec((TMM, D_MODEL), lambda i, te, nu: (i, 0)),
                pl.BlockSpec((None, D_MODEL, D_EXPERT), lambda i, te, nu: (te[i], 0, 0)),
                pl.BlockSpec((None, D_MODEL, D_EXPERT), lambda i, te, nu: (te[i], 0, 0)),
                pl.BlockSpec((None, D_EXPERT, D_MODEL), lambda i, te, nu: (te[i], 0, 0)),
            ],
            out_specs=pl.BlockSpec((TMM, D_MODEL), lambda i, te, nu: (i, 0)),
        ),
        out_shape=jax.ShapeDtypeStruct((ns, D_MODEL), F32),
        compiler_params=_cparams(("arbitrary",)),
        name="moe_ffn",
    )(tile_expert, n_used, xs, wg, wu, wd)


def _combine_kernel(s1_ref, s2_ref, x_ref, rt_ref, mod_ref, ys_hbm, o_ref, buf, sem):
    i = pl.program_id(0)
    nt = pl.num_programs(0)

    def issue(tile, slot):
        for r in range(TM):
            t = tile * TM + r
            _row_copy(ys_hbm, s1_ref[t], buf.at[slot], r, sem.at[slot]).start()
            _row_copy(ys_hbm, s2_ref[t], buf.at[slot], TM + r, sem.at[slot]).start()

    @pl.when(i == 0)
    def _():
        issue(0, 0)

    def step(slot):
        @pl.when(i + 1 < nt)
        def _():
            issue(i + 1, 1 - slot)

        def drain(r, carry):
            _row_copy(ys_hbm, 0, buf.at[slot], 0, sem.at[slot]).wait()
            _row_copy(ys_hbm, 0, buf.at[slot], 0, sem.at[slot]).wait()
            return carry

        lax.fori_loop(0, TM, drain, 0, unroll=8)
        rt = rt_ref[...]
        y = rt[:, 2:3] * buf[slot, 0:TM, :] + rt[:, 3:4] * buf[slot, TM:2 * TM, :]
        o_ref[...] = x_ref[...] + mod_ref[5:6, :] * y

    for slot in range(2):
        @pl.when(i % 2 == slot)
        def _():
            step(slot)


def _combine(xn, rt, mod, ys, s1, s2, n_ctx_tiles):
    n = xn.shape[0]
    kind = lambda i, a, b: (jnp.where(i < n_ctx_tiles, 0, 1), 0, 0)
    return pl.pallas_call(
        _combine_kernel,
        grid_spec=pltpu.PrefetchScalarGridSpec(
            num_scalar_prefetch=2,
            grid=(n // TM,),
            in_specs=[
                pl.BlockSpec((TM, D_MODEL), lambda i, a, b: (i, 0)),
                pl.BlockSpec((TM, SUBLANES), lambda i, a, b: (i, 0)),
                pl.BlockSpec((None, N_MOD, D_MODEL), kind),
                pl.BlockSpec(memory_space=pl.ANY),
            ],
            out_specs=pl.BlockSpec((TM, D_MODEL), lambda i, a, b: (i, 0)),
            scratch_shapes=[
                pltpu.VMEM((2, 2 * TM, D_MODEL), F32),
                pltpu.SemaphoreType.DMA((2,)),
            ],
        ),
        out_shape=jax.ShapeDtypeStruct((n, D_MODEL), F32),
        compiler_params=_cparams(("arbitrary",)),
        name="moe_combine",
    )(s1, s2, xn, rt, mod, ys)


def _moe(xn, h2, r, counts, mod, wg, wu, wd, n_ctx_tiles):
    n = xn.shape[0]
    ns = 2 * n + N_EXPERTS * TMM
    cnt = counts[:, 0].astype(jnp.int32)
    padded = (cnt + TMM - 1) // TMM * TMM
    ends = jnp.cumsum(padded)
    off = ends - padded
    e1 = r[0].astype(jnp.int32)
    e2 = r[1].astype(jnp.int32)
    s1 = jnp.take(off, e1) + r[4].astype(jnp.int32)
    s2 = jnp.take(off, e2) + r[5].astype(jnp.int32)
    tile_start = jnp.arange(ns // TMM, dtype=jnp.int32) * TMM
    tile_expert = jnp.minimum(jnp.searchsorted(ends, tile_start, side='right'), N_EXPERTS - 1).astype(jnp.int32)
    n_used = (ends[-1:] // TMM).astype(jnp.int32)
    last_tile = jnp.where(padded > 0, ends // TMM - 1, -1)
    tile_id = jnp.arange(ns // TMM, dtype=jnp.int32)
    zero_fill = ((tile_id[:, None] == last_tile[None, :]).any(axis=1) | (tile_id >= n_used[0])).astype(jnp.int32)
    xs = _dispatch(h2, s1, s2, zero_fill, ns)
    ys = _ffn(xs, tile_expert, n_used, wg, wu, wd)
    return _combine(xn, r.T, mod, ys, s1, s2, n_ctx_tiles)


def _final_kernel(x_ref, g_ref, o_ref):
    x = x_ref[...]
    o_ref[...] = x * lax.rsqrt(jnp.mean(x * x, axis=-1, keepdims=True) + NORM_EPS) * g_ref[...]


def _final(xc, g, n_ctx_tiles):
    n = xc.shape[0] - n_ctx_tiles * TM
    return pl.pallas_call(
        _final_kernel,
        grid=(n // TM,),
        in_specs=[
            pl.BlockSpec((TM, D_MODEL), lambda i: (i + n_ctx_tiles, 0)),
            pl.BlockSpec((1, D_MODEL), lambda i: (0, 0)),
        ],
        out_specs=pl.BlockSpec((TM, D_MODEL), lambda i: (i, 0)),
        out_shape=jax.ShapeDtypeStruct((n, D_MODEL), F32),
        compiler_params=_cparams(("arbitrary",)),
        name="final_norm",
    )(xc, g)


def kernel(x, c, ctx, c_ctx, ada_w, ada_b, norm_mix_g, norm_ffn_g, w_in, ssm_a_re, ssm_a_im, ssm_log_dt, ssm_b_re, ssm_b_im, ssm_c_re, ssm_c_im, ssm_d, w_a_val, w_a_gate, conv_w, conv_b, conv_ln_g, conv_ln_b, w_b_out, w_out, router_w, router_b, w_exp_gate, w_exp_up, w_exp_down, final_norm_g):
    depth = ada_w.shape[0]
    bsz, seq, _ = x.shape
    n_ctx = ctx.shape[1]
    assert bsz == 1 and n_ctx == TM and seq % TM == 0 and seq % GRID_W == 0
    n_ctx_tiles = n_ctx // TM
    n = n_ctx + seq

    xc = jnp.concatenate([ctx[0], x[0]], axis=0)
    cc = jnp.zeros((SUBLANES, D_MODEL), F32).at[0].set(c_ctx).at[1].set(c[0])
    mods = _mods(cc, ada_w, ada_b)[:, :2].reshape(depth, 2, N_MOD, D_MODEL)

    rwt = router_w.T
    rwh, rwl = _split_bf16(rwt)
    rb = router_b.reshape(N_EXPERTS, 1)
    pm, pmt = _ssm_perm()

    for l in range(depth):
        last = l == depth - 1
        u, z, gates = _pre(xc, mods[l], norm_mix_g[l].reshape(1, -1), w_in[l].astype(BF16), pm,
                           n_ctx_tiles)
        prm = [_ssm_params(ssm_a_re[l, k], ssm_a_im[l, k], ssm_log_dt[l, k], ssm_b_re[l, k],
                           ssm_b_im[l, k], ssm_c_re[l, k], ssm_c_im[l, k]) for k in range(2)]
        ys = _ssm(u, *[jnp.stack(pair) for pair in zip(*prm)], n_ctx_tiles)
        vec = lambda a: a.reshape(1, -1)
        conv_args = (conv_w[l], vec(conv_b[l]), vec(conv_ln_g[l]), vec(conv_ln_b[l]))
        if l % 2 == 0:
            bp = _conv_rows(z, *conv_args, n_ctx_tiles)
        else:
            assert last, "column conv over the context prefix is not needed by the last layer only"
            bp = _conv_cols(z, *conv_args, n_ctx)
        wa = jnp.concatenate([w_a_val[l], w_a_gate[l]], axis=1).astype(BF16)
        xn, h2, r, counts = _merge(xc, u, ys[0], ys[1], bp, gates, mods[l], vec(ssm_d[l]), pmt, wa,
                                   w_b_out[l].astype(BF16), w_out[l].astype(BF16), vec(norm_ffn_g[l]),
                                   rwh, rwl, rb, n_ctx_tiles)
        xc = _moe(xn, h2, r, counts, mods[l], w_exp_gate[l].astype(BF16), w_exp_up[l].astype(BF16),
                  w_exp_down[l].astype(BF16), n_ctx_tiles)
    return _final(xc, vec(final_norm_g), n_ctx_tiles)[None]
```

```python
import functools
import math

import jax
import jax.numpy as jnp
from jax import lax
from jax.experimental import pallas as pl
from jax.experimental.pallas import tpu as pltpu

F32 = jnp.float32
BF16 = jnp.bfloat16

D_MODEL = 1024
GRID_W = 64
SSM_WIDTH = 512
SSM_GROUP = 16
SSM_GROUPS = SSM_WIDTH // SSM_GROUP
SSM_STATE = 64
CONV_WIDTH = 512
CONV_SIZE = 31
CONV_HALF = CONV_SIZE // 2
IN_COLS = SSM_WIDTH + 2 * CONV_WIDTH + 2 * D_MODEL
N_EXPERTS = 16
N_EXPERT_GROUPS = 4
EXPERTS_PER_GROUP = N_EXPERTS // N_EXPERT_GROUPS
D_EXPERT = 512
N_MOD = 6
NORM_EPS = 1e-6

TM = 256
LANES = 128
SUBLANES = 8
SEG = TM // SUBLANES
KB = SSM_WIDTH // LANES
GPB = LANES // SSM_GROUP
SB = GPB * SSM_STATE
PAD = 16
CHUNK = 32
VMEM_LIMIT = 56 * 1024 * 1024


def _cparams(sem):
    return pltpu.CompilerParams(dimension_semantics=sem, vmem_limit_bytes=VMEM_LIMIT)


def _split_bf16(a):
    hi = a.astype(BF16)
    lo = (a - hi.astype(F32)).astype(BF16)
    return hi, lo


def _mods_kernel(c_ref, w_ref, b_ref, o_ref):
    s = c_ref[...]
    s = s * jax.nn.sigmoid(s)
    s_hi, s_lo = _split_bf16(s)
    w_hi, w_lo = _split_bf16(w_ref[...])
    acc = jnp.dot(s_hi, w_hi, preferred_element_type=F32)
    acc += jnp.dot(s_lo, w_hi, preferred_element_type=F32)
    acc += jnp.dot(s_hi, w_lo, preferred_element_type=F32)
    o_ref[...] = acc + b_ref[...]


def _mods(cc, ada_w, ada_b):
    depth = ada_w.shape[0]
    cols = ada_w.shape[2]
    cb = 1536
    return pl.pallas_call(
        _mods_kernel,
        grid=(depth, cols // cb),
        in_specs=[
            pl.BlockSpec((SUBLANES, D_MODEL), lambda l, j: (0, 0)),
            pl.BlockSpec((None, D_MODEL, cb), lambda l, j: (l, 0, j)),
            pl.BlockSpec((None, 1, cb), lambda l, j: (l, 0, j)),
        ],
        out_specs=pl.BlockSpec((None, SUBLANES, cb), lambda l, j: (l, 0, j)),
        out_shape=jax.ShapeDtypeStruct((depth, SUBLANES, cols), F32),
        compiler_params=_cparams(("arbitrary", "arbitrary")),
        name="mods",
    )(cc, ada_w, ada_b.reshape(depth, 1, cols))


def _norm_mod(x, g, shift, scale):
    y = x * lax.rsqrt(jnp.mean(x * x, axis=-1, keepdims=True) + NORM_EPS) * g
    return y * (1.0 + scale) + shift


def _pre_kernel(xa_ref, xb_ref, mod_ref, g_ref, w_ref, pm_ref, u_ref, z_ref, gate_ref, wbf_ref, *,
                n_ctx_tiles):
    @pl.when(pl.program_id(0) == 0)
    def _():
        wbf_ref[...] = w_ref[...].astype(BF16)

    x = jnp.where(pl.program_id(0) < n_ctx_tiles, xa_ref[...], xb_ref[...])
    h = _norm_mod(x, g_ref[...], mod_ref[0:1, :], mod_ref[1:2, :])
    p = jnp.dot(h.astype(BF16), wbf_ref[...], preferred_element_type=F32)
    u_ref[...] = jnp.dot(pm_ref[...], p[:, :SSM_WIDTH].astype(BF16),
                         preferred_element_type=F32).astype(BF16)
    v1 = p[:, SSM_WIDTH:SSM_WIDTH + CONV_WIDTH]
    v2 = p[:, SSM_WIDTH + CONV_WIDTH:SSM_WIDTH + 2 * CONV_WIDTH]
    z_ref[...] = v1 * jax.nn.sigmoid(v2)
    gate_ref[...] = jax.nn.sigmoid(p[:, SSM_WIDTH + 2 * CONV_WIDTH:]).astype(BF16)


def _row_sources(rows_ctx, rows_lat, lat_offset, n_ctx_tiles):
    spec_ctx = pl.BlockSpec((TM, D_MODEL), lambda i, *_: (jnp.minimum(i, n_ctx_tiles - 1), 0))
    spec_lat = pl.BlockSpec((TM, D_MODEL),
                            lambda i, *_: (jnp.maximum(i - n_ctx_tiles, 0) + lat_offset, 0))
    return (rows_ctx, rows_lat), (spec_ctx, spec_lat)


def _pre(rows, n, mod, g, w_in, layer, pm, n_ctx_tiles):
    nt = n // TM
    kind = lambda i: (jnp.where(i < n_ctx_tiles, 0, 1), 0, 0)
    return pl.pallas_call(
        functools.partial(_pre_kernel, n_ctx_tiles=n_ctx_tiles),
        grid=(nt,),
        in_specs=[
            *rows[1],
            pl.BlockSpec((None, N_MOD, D_MODEL), kind),
            pl.BlockSpec((1, D_MODEL), lambda i: (0, 0)),
            pl.BlockSpec((None, D_MODEL, IN_COLS), lambda i: (layer, 0, 0), pipeline_mode=pl.Buffered(1)),
            pl.BlockSpec((TM, TM), lambda i: (0, 0)),
        ],
        out_specs=[
            pl.BlockSpec((TM, SSM_WIDTH), lambda i: (i, 0)),
            pl.BlockSpec((TM, CONV_WIDTH), lambda i: (i, 0)),
            pl.BlockSpec((TM, 2 * D_MODEL), lambda i: (i, 0)),
        ],
        out_shape=[
            jax.ShapeDtypeStruct((n, SSM_WIDTH), BF16),
            jax.ShapeDtypeStruct((n, CONV_WIDTH), F32),
            jax.ShapeDtypeStruct((n, 2 * D_MODEL), BF16),
        ],
        scratch_shapes=[pltpu.VMEM((D_MODEL, IN_COLS), BF16)],
        compiler_params=_cparams(("arbitrary",)),
        name="pre",
    )(*rows[0], mod, g, w_in, pm)


def _ssm_kernel(uf_ref, ub_ref, bw_ref, cw_ref, ar_ref, ai_ref, qr_ref, qi_ref,
                yf_ref, yb_ref, cr_ref, ci_ref, sf_ref, sb_ref, hf_ref, hb_ref):
    @pl.when(pl.program_id(0) == 0)
    def _():
        cr_ref[...] = jnp.zeros_like(cr_ref)
        ci_ref[...] = jnp.zeros_like(ci_ref)

    dirs = (0, 1)
    u_refs = (uf_ref, ub_ref)
    y_refs = (yf_ref, yb_ref)
    s_refs = (sf_ref, sb_ref)
    h_refs = (hf_ref, hb_ref)

    def group(d, i):
        return SEG - 1 - i if d == 1 else i

    def block(k):
        lanes = slice(LANES * k, LANES * (k + 1))
        for d in dirs:
            s_refs[d][...] = jnp.dot(u_refs[d][:, lanes], bw_ref[d, k], preferred_element_type=F32)

        ar = [ar_ref[d, k] for d in dirs]
        ai = [ai_ref[d, k] for d in dirs]

        def advance(carry, i):
            out = []
            for d in dirs:
                rows = pl.ds(pl.multiple_of(group(d, i) * SUBLANES, SUBLANES), SUBLANES)
                hr_d, hi_d = carry[2 * d], carry[2 * d + 1]
                out.append(ar[d] * hr_d - ai[d] * hi_d + s_refs[d][rows, 0:SB])
                out.append(ar[d] * hi_d + ai[d] * hr_d + s_refs[d][rows, SB:2 * SB])
            return tuple(out)

        zero = jnp.zeros((SUBLANES, SB), F32)
        ends = lax.fori_loop(0, SEG, lambda i, c: advance(c, i), (zero,) * 4, unroll=2)
        hr = [ends[0], ends[2]]
        hi = [ends[1], ends[3]]

        hpr = [None, None]
        hpi = [None, None]
        for d in dirs:
            asr = qr_ref[d, k, 0:1, :]
            asi = qi_ref[d, k, 0:1, :]
            pr_ = cr_ref[d, k]
            pi_ = ci_ref[d, k]
            prev_r = [None] * SUBLANES
            prev_i = [None] * SUBLANES
            order = range(SUBLANES - 1, -1, -1) if d == 1 else range(SUBLANES)
            for s in order:
                prev_r[s] = pr_
                prev_i[s] = pi_
                er = hr[d][s:s + 1, :]
                ei = hi[d][s:s + 1, :]
                pr_, pi_ = er + asr * pr_ - asi * pi_, ei + asr * pi_ + asi * pr_
            cr_ref[d, k] = pr_
            ci_ref[d, k] = pi_
            hpr[d] = jnp.concatenate(prev_r, axis=0)
            hpi[d] = jnp.concatenate(prev_i, axis=0)

        def rescan(j, carry):
            first = advance(carry, 2 * j)
            second = advance(first, 2 * j + 1)
            for d in dirs:
                lo, hi_ = (first, second) if d == 0 else (second, first)
                pair = j if d == 0 else SEG // 2 - 1 - j
                rows = pl.ds(pl.multiple_of(pair * 2 * SUBLANES, 2 * SUBLANES), 2 * SUBLANES)
                h_refs[d][rows, 0:SB] = jnp.concatenate([lo[2 * d], hi_[2 * d]], axis=0).astype(BF16)
                h_refs[d][rows, SB:2 * SB] = jnp.concatenate(
                    [lo[2 * d + 1], hi_[2 * d + 1]], axis=0).astype(BF16)
            return second

        lax.fori_loop(0, SEG // 2, rescan, (hpr[0], hpi[0], hpr[1], hpi[1]))

        for d in dirs:
            y_refs[d][:, lanes] = jnp.dot(h_refs[d][...], cw_ref[d, k], preferred_element_type=F32)

    for k in range(KB):
        block(k)


def _ssm(u, bw, cw, ar, ai, qr, qi, n_ctx_tiles):
    n = u.shape[0]
    nt = n // TM

    def fwd(t):
        return (t, 0)

    def bwd(t):
        return (jnp.where(t < n_ctx_tiles, n_ctx_tiles - 1 - t, nt - 1 - (t - n_ctx_tiles)), 0)

    full = lambda a: pl.BlockSpec(a.shape, lambda t: (0,) * a.ndim)
    return pl.pallas_call(
        _ssm_kernel,
        grid=(nt,),
        in_specs=[
            pl.BlockSpec((TM, SSM_WIDTH), fwd),
            pl.BlockSpec((TM, SSM_WIDTH), bwd),
            full(bw), full(cw), full(ar), full(ai), full(qr), full(qi),
        ],
        out_specs=[pl.BlockSpec((TM, SSM_WIDTH), fwd), pl.BlockSpec((TM, SSM_WIDTH), bwd)],
        out_shape=[jax.ShapeDtypeStruct((n, SSM_WIDTH), F32)] * 2,
        scratch_shapes=[
            pltpu.VMEM((2, KB, 1, SB), F32),
            pltpu.VMEM((2, KB, 1, SB), F32),
            pltpu.VMEM((TM, 2 * SB), F32),
            pltpu.VMEM((TM, 2 * SB), F32),
            pltpu.VMEM((TM, 2 * SB), BF16),
            pltpu.VMEM((TM, 2 * SB), BF16),
        ],
        compiler_params=_cparams(("arbitrary",)),
        name="ssm",
    )(u, u, bw, cw, ar, ai, qr, qi)


def _ssm_perm():
    p = jnp.arange(TM)
    src = SEG * (p % SUBLANES) + p // SUBLANES
    pm = (src[:, None] == jnp.arange(TM)[None, :]).astype(BF16)
    return pm, pm.T


def _ssm_params(a_re, a_im, log_dt, b_re, b_im, c_re, c_im):
    lam_re = jnp.minimum(a_re.astype(F32), -1e-4)
    lam_im = a_im.astype(F32)
    dt = jnp.exp(log_dt.astype(F32))[:, None]
    mag = jnp.exp(lam_re * dt)
    abar_re = mag * jnp.cos(lam_im * dt)
    abar_im = mag * jnp.sin(lam_im * dt)
    n_re, n_im = abar_re - 1.0, abar_im
    den = lam_re * lam_re + lam_im * lam_im
    coef_re = (n_re * lam_re + n_im * lam_im) / den
    coef_im = (n_im * lam_re - n_re * lam_im) / den
    br, bi = b_re.astype(F32), b_im.astype(F32)
    bbar_re = coef_re[..., None] * br - coef_im[..., None] * bi
    bbar_im = coef_re[..., None] * bi + coef_im[..., None] * br
    eye = jnp.eye(GPB, dtype=F32)

    def in_block(b):
        b = b.reshape(KB, GPB, SSM_STATE, SSM_GROUP)
        return jnp.einsum('kgpc,gh->kgchp', b, eye).reshape(KB, LANES, SB)

    def out_block(c):
        c = c.reshape(KB, GPB, SSM_GROUP, SSM_STATE)
        return jnp.einsum('kgcp,gh->kgphc', c, eye).reshape(KB, SB, LANES)

    bw = jnp.concatenate([in_block(bbar_re), in_block(bbar_im)], axis=2).astype(BF16)
    cw = jnp.concatenate([out_block(c_re.astype(F32)), -out_block(c_im.astype(F32))], axis=1).astype(BF16)
    qr, qi = abar_re, abar_im
    for _ in range(SEG - 1):
        qr, qi = qr * abar_re - qi * abar_im, qr * abar_im + qi * abar_re
    rep = lambda a: jnp.broadcast_to(a.reshape(KB, 1, SB), (KB, SUBLANES, SB))
    return bw, cw, rep(abar_re), rep(abar_im), rep(qr), rep(qi)


def _ln_silu(acc, g, b):
    mu = jnp.mean(acc, axis=-1, keepdims=True)
    xc = acc - mu
    y = xc * lax.rsqrt(jnp.mean(xc * xc, axis=-1, keepdims=True) + NORM_EPS) * g + b
    return y * jax.nn.sigmoid(y)


def _conv_rows_kernel(z_ref, w_ref, cb_ref, g_ref, b_ref, o_ref, zp_ref, *, n_ctx_tiles):
    i = pl.program_id(0)
    w = w_ref[...]
    cb = cb_ref[...]
    g = g_ref[...]
    b = b_ref[...]

    def run(seq):
        nseq = TM // seq
        stride = seq + 2 * PAD
        total = nseq * stride
        zeros = jnp.zeros((PAD, CONV_WIDTH), F32)
        for a in range(nseq):
            zp_ref[0, stride * a:stride * a + PAD, :] = zeros
            zp_ref[0, stride * a + PAD:stride * a + PAD + seq, :] = z_ref[seq * a:seq * (a + 1), :]
            zp_ref[0, stride * a + PAD + seq:stride * (a + 1), :] = zeros
        for k in range(1, SUBLANES):
            for r0 in range(0, total - SUBLANES, CHUNK):
                rows = min(CHUNK, total - SUBLANES - r0)
                zp_ref[k, r0:r0 + rows, :] = zp_ref[0, r0 + k:r0 + k + rows, :]
        for a in range(nseq):
            for r0 in range(0, seq, CHUNK):
                acc = jnp.zeros((CHUNK, CONV_WIDTH), F32) + cb
                for d in range(CONV_SIZE):
                    off = PAD + d - CONV_HALF
                    start = stride * a + r0 + off // SUBLANES * SUBLANES
                    acc = acc + w[d:d + 1, :] * zp_ref[off % SUBLANES, start:start + CHUNK, :]
                o_ref[seq * a + r0:seq * a + r0 + CHUNK, :] = _ln_silu(acc, g, b).astype(o_ref.dtype)

    @pl.when(i < n_ctx_tiles)
    def _():
        run(TM)

    @pl.when(i >= n_ctx_tiles)
    def _():
        run(GRID_W)


def _conv_rows(z, w, cb, g, b, n_ctx_tiles):
    n = z.shape[0]
    nt = n // TM
    vec = pl.BlockSpec((1, CONV_WIDTH), lambda i: (0, 0))
    return pl.pallas_call(
        functools.partial(_conv_rows_kernel, n_ctx_tiles=n_ctx_tiles),
        grid=(nt,),
        in_specs=[
            pl.BlockSpec((TM, CONV_WIDTH), lambda i: (i, 0)),
            pl.BlockSpec((CONV_SIZE, CONV_WIDTH), lambda i: (0, 0)),
            vec, vec, vec,
        ],
        out_specs=pl.BlockSpec((TM, CONV_WIDTH), lambda i: (i, 0)),
        out_shape=jax.ShapeDtypeStruct((n, CONV_WIDTH), BF16),
        scratch_shapes=[pltpu.VMEM((SUBLANES, TM // GRID_W * (GRID_W + 2 * PAD), CONV_WIDTH), F32)],
        compiler_params=_cparams(("arbitrary",)),
        name="conv_rows",
    )(z, w, cb, g, b)


WB = 16
RB = 16


def _conv_cols_kernel(z_ref, w_ref, cb_ref, g_ref, b_ref, o_ref, zp_ref, *, r_ctx, rows):
    w = w_ref[...]
    cb = cb_ref[...]
    g = g_ref[...]
    b = b_ref[...]
    zeros = jnp.zeros((CONV_HALF, WB, CONV_WIDTH), F32)
    zp_ref[0:CONV_HALF] = zeros
    zp_ref[CONV_HALF + rows:CONV_HALF + rows + CONV_HALF] = zeros
    zp_ref[CONV_HALF:CONV_HALF + rows] = z_ref[r_ctx:r_ctx + rows]
    o_ref[0:r_ctx] = jnp.zeros((r_ctx, WB, CONV_WIDTH), o_ref.dtype)

    def column(h, c):
        hs = slice(SUBLANES * h, SUBLANES * (h + 1))
        ls = slice(LANES * c, LANES * (c + 1))
        wv = [jnp.broadcast_to(w[d:d + 1, ls], (SUBLANES, LANES)) for d in range(CONV_SIZE)]
        bias = jnp.broadcast_to(cb[:, ls], (SUBLANES, LANES))

        def body(blk, carry):
            r0 = blk * RB
            acc = [bias] * RB
            for q in range(RB + CONV_SIZE - 1):
                zq = zp_ref[r0 + q, hs, ls]
                for j in range(RB):
                    d = q - j
                    if 0 <= d < CONV_SIZE:
                        acc[j] = acc[j] + wv[d] * zq
            for j in range(RB):
                zp_ref[r0 + j, hs, ls] = acc[j]
            return carry

        lax.fori_loop(0, rows // RB, body, 0)

    for h in range(WB // SUBLANES):
        for c in range(CONV_WIDTH // LANES):
            column(h, c)

    def norm(blk, carry):
        r0 = blk * SUBLANES
        v = zp_ref[pl.ds(r0, SUBLANES)].reshape(SUBLANES * WB, CONV_WIDTH)
        y = _ln_silu(v, g, b).astype(o_ref.dtype)
        o_ref[pl.ds(r_ctx + r0, SUBLANES)] = y.reshape(SUBLANES, WB, CONV_WIDTH)
        return carry

    lax.fori_loop(0, rows // SUBLANES, norm, 0)


def _conv_cols(z, w, cb, g, b, n_ctx):
    n = z.shape[0]
    r_all = n // GRID_W
    r_ctx = n_ctx // GRID_W
    rows = r_all - r_ctx
    z3 = z.reshape(r_all, GRID_W, CONV_WIDTH)
    vec = pl.BlockSpec((1, CONV_WIDTH), lambda j: (0, 0))
    out = pl.pallas_call(
        functools.partial(_conv_cols_kernel, r_ctx=r_ctx, rows=rows),
        grid=(GRID_W // WB,),
        in_specs=[
            pl.BlockSpec((r_all, WB, CONV_WIDTH), lambda j: (0, j, 0)),
            pl.BlockSpec((CONV_SIZE, CONV_WIDTH), lambda j: (0, 0)),
            vec, vec, vec,
        ],
        out_specs=pl.BlockSpec((r_all, WB, CONV_WIDTH), lambda j: (0, j, 0)),
        out_shape=jax.ShapeDtypeStruct((r_all, GRID_W, CONV_WIDTH), BF16),
        scratch_shapes=[pltpu.VMEM((rows + 2 * CONV_HALF, WB, CONV_WIDTH), F32)],
        compiler_params=_cparams(("arbitrary",)),
        name="conv_cols",
    )(z3, w, cb, g, b)
    return out.reshape(n, CONV_WIDTH)


def _route(lt):
    l = [lt[e:e + 1, :] for e in range(N_EXPERTS)]
    m = l[0]
    for e in range(1, N_EXPERTS):
        m = jnp.maximum(m, l[e])
    ex = [jnp.exp(v - m) for v in l]
    den = ex[0]
    for e in range(1, N_EXPERTS):
        den = den + ex[e]
    p = [v / den for v in ex]
    scores = []
    for gidx in range(N_EXPERT_GROUPS):
        q = p[gidx * EXPERTS_PER_GROUP:(gidx + 1) * EXPERTS_PER_GROUP]
        best = None
        for a in range(EXPERTS_PER_GROUP):
            for c in range(a + 1, EXPERTS_PER_GROUP):
                pair = q[a] + q[c]
                best = pair if best is None else jnp.maximum(best, pair)
        scores.append(best)
    bs = scores[0]
    bg = jnp.zeros_like(bs)
    for gidx in range(1, N_EXPERT_GROUPS):
        better = scores[gidx] > bs
        bs = jnp.where(better, scores[gidx], bs)
        bg = jnp.where(better, float(gidx), bg)
    q = []
    for a in range(EXPERTS_PER_GROUP):
        v = p[a]
        for gidx in range(1, N_EXPERT_GROUPS):
            v = jnp.where(bg == float(gidx), p[gidx * EXPERTS_PER_GROUP + a], v)
        q.append(v)
    p1 = q[0]
    i1 = jnp.zeros_like(p1)
    for a in range(1, EXPERTS_PER_GROUP):
        better = q[a] > p1
        p1 = jnp.where(better, q[a], p1)
        i1 = jnp.where(better, float(a), i1)
    p2 = jnp.full_like(p1, -1.0)
    i2 = jnp.zeros_like(p1)
    for a in range(EXPERTS_PER_GROUP):
        better = (q[a] > p2) & (i1 != float(a))
        p2 = jnp.where(better, q[a], p2)
        i2 = jnp.where(better, float(a), i2)
    tot = p1 + p2
    base = bg * float(EXPERTS_PER_GROUP)
    return base + i1, base + i2, p1 / tot, p2 / tot


def _merge_kernel(xa_ref, xb_ref, u_ref, yf_ref, yb_ref, bp_ref, gate_ref, mod_ref, d_ref, pmt_ref,
                  wav_ref, wag_ref, wb_ref, wo_ref, gf_ref, rwh_ref, rwl_ref, rb_ref,
                  xo_ref, h2_ref, r_ref, c_ref, cnt_ref, wa_bf, wb_bf, wo_bf, *, n_ctx_tiles):
    @pl.when(pl.program_id(0) == 0)
    def _():
        wa_bf[:, :D_MODEL] = wav_ref[...].astype(BF16)
        wa_bf[:, D_MODEL:] = wag_ref[...].astype(BF16)
        wb_bf[...] = wb_ref[...].astype(BF16)
        wo_bf[...] = wo_ref[...].astype(BF16)

    ya = jax.nn.gelu(yf_ref[...] + yb_ref[...] + d_ref[...] * u_ref[...].astype(F32), approximate=True)
    ya = jnp.dot(pmt_ref[...], ya.astype(BF16), preferred_element_type=F32).astype(BF16)
    av = jnp.dot(ya, wa_bf[...], preferred_element_type=F32)
    a_out = av[:, :D_MODEL] * jax.nn.sigmoid(av[:, D_MODEL:])
    b_out = jnp.dot(bp_ref[...], wb_bf[...], preferred_element_type=F32)
    gate = gate_ref[...].astype(F32)
    merged = gate[:, :D_MODEL] * a_out + gate[:, D_MODEL:] * b_out
    mx = jnp.dot(merged.astype(BF16), wo_bf[...], preferred_element_type=F32)
    x = jnp.where(pl.program_id(0) < n_ctx_tiles, xa_ref[...], xb_ref[...])
    xn = x + mod_ref[2:3, :] * mx
    xo_ref[...] = xn
    h2 = _norm_mod(xn, gf_ref[...], mod_ref[3:4, :], mod_ref[4:5, :])
    h2_ref[...] = h2
    h_hi, h_lo = _split_bf16(h2)
    nt_dims = (((1,), (1,)), ((), ()))
    lt = lax.dot_general(rwh_ref[...], h_hi, nt_dims, preferred_element_type=F32)
    lt += lax.dot_general(rwh_ref[...], h_lo, nt_dims, preferred_element_type=F32)
    lt += lax.dot_general(rwl_ref[...], h_hi, nt_dims, preferred_element_type=F32)
    lt = lt + rb_ref[...]
    e1, e2, w1, w2 = _route(lt)
    @pl.when(pl.program_id(0) == 0)
    def _():
        cnt_ref[...] = jnp.zeros_like(cnt_ref)

    eid = lax.broadcasted_iota(jnp.int32, (N_EXPERTS, TM), 0).astype(F32)
    hit1 = eid == e1
    hit2 = eid == e2
    onehot = jnp.where(hit1 | hit2, 1.0, 0.0)
    before = lax.broadcasted_iota(jnp.int32, (TM, TM), 0) < lax.broadcasted_iota(jnp.int32, (TM, TM), 1)
    excl = jnp.dot(onehot.astype(BF16), jnp.where(before, 1.0, 0.0).astype(BF16),
                   preferred_element_type=F32)
    seen = excl + cnt_ref[:, 0:1]
    r1 = jnp.sum(jnp.where(hit1, seen, 0.0), axis=0, keepdims=True)
    r2 = jnp.sum(jnp.where(hit2, seen, 0.0), axis=0, keepdims=True)
    cnt_ref[...] = cnt_ref[...] + jnp.sum(onehot, axis=1, keepdims=True)
    c_ref[...] = cnt_ref[...]
    zero = jnp.zeros_like(e1)
    r_ref[...] = jnp.concatenate([e1, e2, w1, w2, r1, r2, zero, zero], axis=0)


def _merge(rows, n, u, yf, yb, bp, gates, mod, d_skip, pmt, w_a_val, w_a_gate, w_b_out, w_out, layer,
           gf, rwh, rwl, rb, n_ctx_tiles):
    nt = n // TM
    kind = lambda i: (jnp.where(i < n_ctx_tiles, 0, 1), 0, 0)
    tile = lambda w: pl.BlockSpec((TM, w), lambda i: (i, 0))
    full = lambda a: pl.BlockSpec(a.shape, lambda i: (0,) * a.ndim)
    lay = lambda a: pl.BlockSpec((None,) + a.shape[1:], lambda i: (layer, 0, 0))
    return pl.pallas_call(
        functools.partial(_merge_kernel, n_ctx_tiles=n_ctx_tiles),
        grid=(nt,),
        in_specs=[
            *rows[1], tile(SSM_WIDTH), tile(SSM_WIDTH), tile(SSM_WIDTH), tile(CONV_WIDTH),
            tile(2 * D_MODEL),
            pl.BlockSpec((None, N_MOD, D_MODEL), kind),
            full(d_skip), full(pmt), lay(w_a_val), lay(w_a_gate), lay(w_b_out), lay(w_out), full(gf),
            full(rwh), full(rwl), full(rb),
        ],
        out_specs=[
            tile(D_MODEL), tile(D_MODEL),
            pl.BlockSpec((SUBLANES, TM), lambda i: (0, i)),
            pl.BlockSpec((N_EXPERTS, LANES), lambda i: (0, 0)),
        ],
        out_shape=[
            jax.ShapeDtypeStruct((n, D_MODEL), F32),
            jax.ShapeDtypeStruct((n, D_MODEL), F32),
            jax.ShapeDtypeStruct((SUBLANES, n), F32),
            jax.ShapeDtypeStruct((N_EXPERTS, LANES), F32),
        ],
        scratch_shapes=[
            pltpu.VMEM((N_EXPERTS, LANES), F32),
            pltpu.VMEM((SSM_WIDTH, 2 * D_MODEL), BF16),
            pltpu.VMEM((CONV_WIDTH, D_MODEL), BF16),
            pltpu.VMEM((D_MODEL, D_MODEL), BF16),
        ],
        compiler_params=_cparams(("arbitrary",)),
        name="merge",
    )(*rows[0], u, yf, yb, bp, gates, mod, d_skip, pmt, w_a_val, w_a_gate, w_b_out, w_out, gf, rwh, rwl, rb)


TMM = 256


def _row_copy(src, i, dst, j, sem):
    return pltpu.make_async_copy(src.at[pl.ds(i, 1)], dst.at[pl.ds(j, 1)], sem)


def _invert_kernel(s1_ref, s2_ref, inv_ref, *, n):
    def clear(j, carry):
        inv_ref[j] = 0
        return carry

    lax.fori_loop(0, inv_ref.shape[0], clear, 0, unroll=8)

    def put(t, carry):
        inv_ref[s1_ref[t]] = t
        inv_ref[s2_ref[t]] = t
        return carry

    lax.fori_loop(0, n, put, 0, unroll=8)


def _invert(s1, s2, ns):
    return pl.pallas_call(
        functools.partial(_invert_kernel, n=s1.shape[0]),
        grid_spec=pltpu.PrefetchScalarGridSpec(
            num_scalar_prefetch=2,
            grid=(1,),
            in_specs=[],
            out_specs=pl.BlockSpec(memory_space=pltpu.SMEM),
        ),
        out_shape=jax.ShapeDtypeStruct((ns,), jnp.int32),
        compiler_params=_cparams(("arbitrary",)),
        name="moe_invert",
    )(s1, s2)


def _ffn_kernel(te_ref, nu_ref, inv_ref, h_hbm, wg_ref, wu_ref, wd_ref, y_ref, xbuf, sem):
    i = pl.program_id(0)
    nu = nu_ref[0]

    def issue(tile, slot):
        for r in range(TMM):
            _row_copy(h_hbm, inv_ref[tile * TMM + r], xbuf.at[slot], r, sem.at[slot]).start()

    @pl.when((i == 0) & (nu > 0))
    def _():
        issue(0, 0)

    def step(slot):
        @pl.when(i + 1 < nu)
        def _():
            issue(i + 1, 1 - slot)

        def drain(r, carry):
            _row_copy(h_hbm, 0, xbuf.at[slot], 0, sem.at[slot]).wait()
            return carry

        lax.fori_loop(0, TMM, drain, 0, unroll=8)
        x = xbuf[slot].astype(BF16)
        hid = jnp.dot(x, wg_ref[...].astype(BF16), preferred_element_type=F32)
        hid = hid * jax.nn.sigmoid(hid) * jnp.dot(x, wu_ref[...].astype(BF16), preferred_element_type=F32)
        y_ref[...] = jnp.dot(hid.astype(BF16), wd_ref[...].astype(BF16), preferred_element_type=F32)

    for slot in range(2):
        @pl.when((i % 2 == slot) & (i < nu))
        def _():
            step(slot)

    @pl.when(i >= nu)
    def _():
        y_ref[...] = jnp.zeros_like(y_ref)


def _ffn(h2, inv, tile_expert, n_used, wg, wu, wd, layer):
    ns = inv.shape[0]
    wspec = lambda *shape: pl.BlockSpec((None, None) + shape, lambda i, te, nu, iv: (layer, te[i], 0, 0))
    return pl.pallas_call(
        _ffn_kernel,
        grid_spec=pltpu.PrefetchScalarGridSpec(
            num_scalar_prefetch=3,
            grid=(ns // TMM,),
            in_specs=[
                pl.BlockSpec(memory_space=pl.ANY),
                wspec(D_MODEL, D_EXPERT), wspec(D_MODEL, D_EXPERT), wspec(D_EXPERT, D_MODEL),
            ],
            out_specs=pl.BlockSpec((TMM, D_MODEL), lambda i, te, nu, iv: (i, 0)),
            scratch_shapes=[pltpu.VMEM((2, TMM, D_MODEL), F32), pltpu.SemaphoreType.DMA((2,))],
        ),
        out_shape=jax.ShapeDtypeStruct((ns, D_MODEL), F32),
        compiler_params=_cparams(("arbitrary",)),
        name="moe_ffn",
    )(tile_expert, n_used, inv, h2, wg, wu, wd)


def _combine_kernel(s1_ref, s2_ref, x_ref, rt_ref, mod_ref, g_ref, ys_hbm, o_ref, buf, sem, *,
                    first_tile, final):
    i = pl.program_id(0)
    nt = pl.num_programs(0)

    def issue(tile, slot):
        for r in range(TM):
            t = (first_tile + tile) * TM + r
            _row_copy(ys_hbm, s1_ref[t], buf.at[slot], r, sem.at[slot]).start()
            _row_copy(ys_hbm, s2_ref[t], buf.at[slot], TM + r, sem.at[slot]).start()

    @pl.when(i == 0)
    def _():
        issue(0, 0)

    def step(slot):
        @pl.when(i + 1 < nt)
        def _():
            issue(i + 1, 1 - slot)

        def drain(r, carry):
            _row_copy(ys_hbm, 0, buf.at[slot], 0, sem.at[slot]).wait()
            _row_copy(ys_hbm, 0, buf.at[slot], 0, sem.at[slot]).wait()
            return carry

        lax.fori_loop(0, TM, drain, 0, unroll=8)
        rt = rt_ref[...]
        y = rt[:, 2:3] * buf[slot, 0:TM, :] + rt[:, 3:4] * buf[slot, TM:2 * TM, :]
        out = x_ref[...] + mod_ref[5:6, :] * y
        if final:
            out = out * lax.rsqrt(jnp.mean(out * out, axis=-1, keepdims=True) + NORM_EPS) * g_ref[...]
        o_ref[...] = out

    for slot in range(2):
        @pl.when(i % 2 == slot)
        def _():
            step(slot)


def _combine(xn, rt, mod, g, ys, s1, s2, n_ctx_tiles, final):
    n = xn.shape[0]
    first = n_ctx_tiles if final else 0
    tiles = n // TM - first
    kind = lambda i, a, b: (jnp.where(i + first < n_ctx_tiles, 0, 1), 0, 0)
    return pl.pallas_call(
        functools.partial(_combine_kernel, first_tile=first, final=final),
        grid_spec=pltpu.PrefetchScalarGridSpec(
            num_scalar_prefetch=2,
            grid=(tiles,),
            in_specs=[
                pl.BlockSpec((TM, D_MODEL), lambda i, a, b: (i + first, 0)),
                pl.BlockSpec((TM, SUBLANES), lambda i, a, b: (i + first, 0)),
                pl.BlockSpec((None, N_MOD, D_MODEL), kind),
                pl.BlockSpec((1, D_MODEL), lambda i, a, b: (0, 0)),
                pl.BlockSpec(memory_space=pl.ANY),
            ],
            out_specs=pl.BlockSpec((TM, D_MODEL), lambda i, a, b: (i, 0)),
            scratch_shapes=[
                pltpu.VMEM((2, 2 * TM, D_MODEL), F32),
                pltpu.SemaphoreType.DMA((2,)),
            ],
        ),
        out_shape=jax.ShapeDtypeStruct((tiles * TM, D_MODEL), F32),
        compiler_params=_cparams(("arbitrary",)),
        name="moe_combine",
    )(s1, s2, xn, rt, mod, g, ys)


def _moe(xn, h2, r, counts, mod, wg, wu, wd, layer, final_g, n_ctx_tiles, final):
    n = xn.shape[0]
    ns = 2 * n + N_EXPERTS * TMM
    cnt = counts[:, 0].astype(jnp.int32)
    padded = (cnt + TMM - 1) // TMM * TMM
    ends = jnp.cumsum(padded)
    off = ends - padded
    experts = jnp.arange(N_EXPERTS, dtype=jnp.int32)

    def slot(e_row, rank_row):
        hit = e_row.astype(jnp.int32)[:, None] == experts[None, :]
        return jnp.sum(jnp.where(hit, off[None, :], 0), axis=1) + rank_row.astype(jnp.int32)

    s1 = slot(r[0], r[4])
    s2 = slot(r[1], r[5])
    tile_start = jnp.arange(ns // TMM, dtype=jnp.int32) * TMM
    tile_expert = jnp.minimum(jnp.sum(ends[None, :] <= tile_start[:, None], axis=1),
                              N_EXPERTS - 1).astype(jnp.int32)
    n_used = (ends[-1:] // TMM).astype(jnp.int32)
    inv = _invert(s1, s2, ns)
    ys = _ffn(h2, inv, tile_expert, n_used, wg, wu, wd, layer)
    return _combine(xn, r.T, mod, final_g, ys, s1, s2, n_ctx_tiles, final)


def kernel(x, c, ctx, c_ctx, ada_w, ada_b, norm_mix_g, norm_ffn_g, w_in, ssm_a_re, ssm_a_im, ssm_log_dt, ssm_b_re, ssm_b_im, ssm_c_re, ssm_c_im, ssm_d, w_a_val, w_a_gate, conv_w, conv_b, conv_ln_g, conv_ln_b, w_b_out, w_out, router_w, router_b, w_exp_gate, w_exp_up, w_exp_down, final_norm_g):
    depth = ada_w.shape[0]
    bsz, seq, _ = x.shape
    n_ctx = ctx.shape[1]
    assert bsz == 1 and n_ctx == TM and seq % TM == 0 and seq % GRID_W == 0
    n_ctx_tiles = n_ctx // TM
    n = n_ctx + seq

    rows = _row_sources(ctx[0], x[0], 0, n_ctx_tiles)
    cc = jnp.zeros((SUBLANES, D_MODEL), F32).at[0].set(c_ctx).at[1].set(c[0])
    mods = _mods(cc, ada_w, ada_b)[:, :2].reshape(depth, 2, N_MOD, D_MODEL)

    rwt = router_w.T
    rwh, rwl = _split_bf16(rwt)
    rb = router_b.reshape(N_EXPERTS, 1)
    pm, pmt = _ssm_perm()

    for l in range(depth):
        last = l == depth - 1
        u, z, gates = _pre(rows, n, mods[l], norm_mix_g[l].reshape(1, -1), w_in, l, pm, n_ctx_tiles)
        prm = [_ssm_params(ssm_a_re[l, k], ssm_a_im[l, k], ssm_log_dt[l, k], ssm_b_re[l, k],
                           ssm_b_im[l, k], ssm_c_re[l, k], ssm_c_im[l, k]) for k in range(2)]
        ys = _ssm(u, *[jnp.stack(pair) for pair in zip(*prm)], n_ctx_tiles)
        vec = lambda a: a.reshape(1, -1)
        conv_args = (conv_w[l], vec(conv_b[l]), vec(conv_ln_g[l]), vec(conv_ln_b[l]))
        if l % 2 == 0:
            bp = _conv_rows(z, *conv_args, n_ctx_tiles)
        else:
            assert last, "column conv over the context prefix is not needed by the last layer only"
            bp = _conv_cols(z, *conv_args, n_ctx)
        xn, h2, r, counts = _merge(rows, n, u, ys[0], ys[1], bp, gates, mods[l], vec(ssm_d[l]), pmt,
                                   w_a_val, w_a_gate, w_b_out, w_out, l, vec(norm_ffn_g[l]),
                                   rwh, rwl, rb, n_ctx_tiles)
        xc = _moe(xn, h2, r, counts, mods[l], w_exp_gate, w_exp_up, w_exp_down, l,
                  vec(final_norm_g), n_ctx_tiles, final=last)
        rows = _row_sources(xc, xc, n_ctx_tiles, n_ctx_tiles)
    return xc[None]
```

```python
import functools
import math

import jax
import jax.numpy as jnp
from jax import lax
from jax.experimental import pallas as pl
from jax.experimental.pallas import tpu as pltpu

F32 = jnp.float32
BF16 = jnp.bfloat16

D_MODEL = 1024
GRID_W = 64
SSM_WIDTH = 512
SSM_GROUP = 16
SSM_GROUPS = SSM_WIDTH // SSM_GROUP
SSM_STATE = 64
CONV_WIDTH = 512
CONV_SIZE = 31
CONV_HALF = CONV_SIZE // 2
IN_COLS = SSM_WIDTH + 2 * CONV_WIDTH + 2 * D_MODEL
N_EXPERTS = 16
N_EXPERT_GROUPS = 4
EXPERTS_PER_GROUP = N_EXPERTS // N_EXPERT_GROUPS
D_EXPERT = 512
N_MOD = 6
NORM_EPS = 1e-6

TM = 256
LANES = 128
SUBLANES = 8
SEG = TM // SUBLANES
KB = SSM_WIDTH // LANES
GPB = LANES // SSM_GROUP
SB = GPB * SSM_STATE
PAD = 16
CHUNK = 32
VMEM_LIMIT = 56 * 1024 * 1024


def _cparams(sem):
    return pltpu.CompilerParams(dimension_semantics=sem, vmem_limit_bytes=VMEM_LIMIT)


def _split_bf16(a):
    hi = a.astype(BF16)
    lo = (a - hi.astype(F32)).astype(BF16)
    return hi, lo


def _mods_kernel(c_ref, w_ref, b_ref, o_ref):
    s = c_ref[...]
    s = s * jax.nn.sigmoid(s)
    s_hi, s_lo = _split_bf16(s)
    w_hi, w_lo = _split_bf16(w_ref[...])
    acc = jnp.dot(s_hi, w_hi, preferred_element_type=F32)
    acc += jnp.dot(s_lo, w_hi, preferred_element_type=F32)
    acc += jnp.dot(s_hi, w_lo, preferred_element_type=F32)
    o_ref[...] = acc + b_ref[...]


def _mods(cc, ada_w, ada_b):
    depth = ada_w.shape[0]
    cols = ada_w.shape[2]
    cb = 1536
    return pl.pallas_call(
        _mods_kernel,
        grid=(depth, cols // cb),
        in_specs=[
            pl.BlockSpec((SUBLANES, D_MODEL), lambda l, j: (0, 0)),
            pl.BlockSpec((None, D_MODEL, cb), lambda l, j: (l, 0, j)),
            pl.BlockSpec((None, 1, cb), lambda l, j: (l, 0, j)),
        ],
        out_specs=pl.BlockSpec((None, SUBLANES, cb), lambda l, j: (l, 0, j)),
        out_shape=jax.ShapeDtypeStruct((depth, SUBLANES, cols), F32),
        compiler_params=_cparams(("arbitrary", "arbitrary")),
        name="mods",
    )(cc, ada_w, ada_b.reshape(depth, 1, cols))


def _norm_mod(x, g, shift, scale):
    y = x * lax.rsqrt(jnp.mean(x * x, axis=-1, keepdims=True) + NORM_EPS) * g
    return y * (1.0 + scale) + shift


def _pre_kernel(xa_ref, xb_ref, mod_ref, g_ref, w_ref, pm_ref, u_ref, z_ref, gate_ref, wbf_ref, *,
                n_ctx_tiles):
    @pl.when(pl.program_id(0) == 0)
    def _():
        wbf_ref[...] = w_ref[...].astype(BF16)

    x = jnp.where(pl.program_id(0) < n_ctx_tiles, xa_ref[...], xb_ref[...])
    h = _norm_mod(x, g_ref[...], mod_ref[0:1, :], mod_ref[1:2, :])
    p = jnp.dot(h.astype(BF16), wbf_ref[...], preferred_element_type=F32)
    u_ref[...] = jnp.dot(pm_ref[...], p[:, :SSM_WIDTH].astype(BF16),
                         preferred_element_type=F32).astype(BF16)
    v1 = p[:, SSM_WIDTH:SSM_WIDTH + CONV_WIDTH]
    v2 = p[:, SSM_WIDTH + CONV_WIDTH:SSM_WIDTH + 2 * CONV_WIDTH]
    z_ref[...] = v1 * jax.nn.sigmoid(v2)
    gate_ref[...] = jax.nn.sigmoid(p[:, SSM_WIDTH + 2 * CONV_WIDTH:]).astype(BF16)


def _row_sources(rows_ctx, rows_lat, lat_offset, n_ctx_tiles):
    spec_ctx = pl.BlockSpec((TM, D_MODEL), lambda i, *_: (jnp.minimum(i, n_ctx_tiles - 1), 0))
    spec_lat = pl.BlockSpec((TM, D_MODEL),
                            lambda i, *_: (jnp.maximum(i - n_ctx_tiles, 0) + lat_offset, 0))
    return (rows_ctx, rows_lat), (spec_ctx, spec_lat)


def _pre(rows, n, mod, g, w_in, layer, pm, n_ctx_tiles):
    nt = n // TM
    kind = lambda i: (jnp.where(i < n_ctx_tiles, 0, 1), 0, 0)
    return pl.pallas_call(
        functools.partial(_pre_kernel, n_ctx_tiles=n_ctx_tiles),
        grid=(nt,),
        in_specs=[
            *rows[1],
            pl.BlockSpec((None, N_MOD, D_MODEL), kind),
            pl.BlockSpec((1, D_MODEL), lambda i: (0, 0)),
            pl.BlockSpec((None, D_MODEL, IN_COLS), lambda i: (layer, 0, 0), pipeline_mode=pl.Buffered(1)),
            pl.BlockSpec((TM, TM), lambda i: (0, 0)),
        ],
        out_specs=[
            pl.BlockSpec((TM, SSM_WIDTH), lambda i: (i, 0)),
            pl.BlockSpec((TM, CONV_WIDTH), lambda i: (i, 0)),
            pl.BlockSpec((TM, 2 * D_MODEL), lambda i: (i, 0)),
        ],
        out_shape=[
            jax.ShapeDtypeStruct((n, SSM_WIDTH), BF16),
            jax.ShapeDtypeStruct((n, CONV_WIDTH), F32),
            jax.ShapeDtypeStruct((n, 2 * D_MODEL), BF16),
        ],
        scratch_shapes=[pltpu.VMEM((D_MODEL, IN_COLS), BF16)],
        compiler_params=_cparams(("arbitrary",)),
        name="pre",
    )(*rows[0], mod, g, w_in, pm)


def _ssm_kernel(uf_ref, ub_ref, bw_ref, cw_ref, ar_ref, ai_ref, qr_ref, qi_ref,
                yf_ref, yb_ref, cr_ref, ci_ref, sf_ref, sb_ref, hf_ref, hb_ref):
    @pl.when(pl.program_id(0) == 0)
    def _():
        cr_ref[...] = jnp.zeros_like(cr_ref)
        ci_ref[...] = jnp.zeros_like(ci_ref)

    dirs = (0, 1)
    u_refs = (uf_ref, ub_ref)
    y_refs = (yf_ref, yb_ref)
    s_refs = (sf_ref, sb_ref)
    h_refs = (hf_ref, hb_ref)

    def group(d, i):
        return SEG - 1 - i if d == 1 else i

    def block(k):
        lanes = slice(LANES * k, LANES * (k + 1))
        for d in dirs:
            s_refs[d][...] = jnp.dot(u_refs[d][:, lanes], bw_ref[d, k], preferred_element_type=F32)

        ar = [ar_ref[d, k] for d in dirs]
        ai = [ai_ref[d, k] for d in dirs]

        def advance(carry, i):
            out = []
            for d in dirs:
                rows = pl.ds(pl.multiple_of(group(d, i) * SUBLANES, SUBLANES), SUBLANES)
                hr_d, hi_d = carry[2 * d], carry[2 * d + 1]
                out.append(ar[d] * hr_d - ai[d] * hi_d + s_refs[d][rows, 0:SB])
                out.append(ar[d] * hi_d + ai[d] * hr_d + s_refs[d][rows, SB:2 * SB])
            return tuple(out)

        zero = jnp.zeros((SUBLANES, SB), F32)
        ends = lax.fori_loop(0, SEG, lambda i, c: advance(c, i), (zero,) * 4, unroll=2)
        hr = [ends[0], ends[2]]
        hi = [ends[1], ends[3]]

        hpr = [None, None]
        hpi = [None, None]
        for d in dirs:
            asr = qr_ref[d, k, 0:1, :]
            asi = qi_ref[d, k, 0:1, :]
            pr_ = cr_ref[d, k]
            pi_ = ci_ref[d, k]
            prev_r = [None] * SUBLANES
            prev_i = [None] * SUBLANES
            order = range(SUBLANES - 1, -1, -1) if d == 1 else range(SUBLANES)
            for s in order:
                prev_r[s] = pr_
                prev_i[s] = pi_
                er = hr[d][s:s + 1, :]
                ei = hi[d][s:s + 1, :]
                pr_, pi_ = er + asr * pr_ - asi * pi_, ei + asr * pi_ + asi * pr_
            cr_ref[d, k] = pr_
            ci_ref[d, k] = pi_
            hpr[d] = jnp.concatenate(prev_r, axis=0)
            hpi[d] = jnp.concatenate(prev_i, axis=0)

        def rescan(j, carry):
            first = advance(carry, 2 * j)
            second = advance(first, 2 * j + 1)
            for d in dirs:
                lo, hi_ = (first, second) if d == 0 else (second, first)
                pair = j if d == 0 else SEG // 2 - 1 - j
                rows = pl.ds(pl.multiple_of(pair * 2 * SUBLANES, 2 * SUBLANES), 2 * SUBLANES)
                h_refs[d][rows, 0:SB] = jnp.concatenate([lo[2 * d], hi_[2 * d]], axis=0).astype(BF16)
                h_refs[d][rows, SB:2 * SB] = jnp.concatenate(
                    [lo[2 * d + 1], hi_[2 * d + 1]], axis=0).astype(BF16)
            return second

        lax.fori_loop(0, SEG // 2, rescan, (hpr[0], hpi[0], hpr[1], hpi[1]))

        for d in dirs:
            y_refs[d][:, lanes] = jnp.dot(h_refs[d][...], cw_ref[d, k], preferred_element_type=F32)

    for k in range(KB):
        block(k)


def _ssm(u, bw, cw, ar, ai, qr, qi, n_ctx_tiles):
    n = u.shape[0]
    nt = n // TM

    def fwd(t):
        return (t, 0)

    def bwd(t):
        return (jnp.where(t < n_ctx_tiles, n_ctx_tiles - 1 - t, nt - 1 - (t - n_ctx_tiles)), 0)

    full = lambda a: pl.BlockSpec(a.shape, lambda t: (0,) * a.ndim)
    return pl.pallas_call(
        _ssm_kernel,
        grid=(nt,),
        in_specs=[
            pl.BlockSpec((TM, SSM_WIDTH), fwd),
            pl.BlockSpec((TM, SSM_WIDTH), bwd),
            full(bw), full(cw), full(ar), full(ai), full(qr), full(qi),
        ],
        out_specs=[pl.BlockSpec((TM, SSM_WIDTH), fwd), pl.BlockSpec((TM, SSM_WIDTH), bwd)],
        out_shape=[jax.ShapeDtypeStruct((n, SSM_WIDTH), F32)] * 2,
        scratch_shapes=[
            pltpu.VMEM((2, KB, 1, SB), F32),
            pltpu.VMEM((2, KB, 1, SB), F32),
            pltpu.VMEM((TM, 2 * SB), F32),
            pltpu.VMEM((TM, 2 * SB), F32),
            pltpu.VMEM((TM, 2 * SB), BF16),
            pltpu.VMEM((TM, 2 * SB), BF16),
        ],
        compiler_params=_cparams(("arbitrary",)),
        name="ssm",
    )(u, u, bw, cw, ar, ai, qr, qi)


def _ssm_perm():
    p = jnp.arange(TM)
    src = SEG * (p % SUBLANES) + p // SUBLANES
    pm = (src[:, None] == jnp.arange(TM)[None, :]).astype(BF16)
    return pm, pm.T


def _ssm_params(a_re, a_im, log_dt, b_re, b_im, c_re, c_im):
    lam_re = jnp.minimum(a_re.astype(F32), -1e-4)
    lam_im = a_im.astype(F32)
    dt = jnp.exp(log_dt.astype(F32))[:, None]
    mag = jnp.exp(lam_re * dt)
    abar_re = mag * jnp.cos(lam_im * dt)
    abar_im = mag * jnp.sin(lam_im * dt)
    n_re, n_im = abar_re - 1.0, abar_im
    den = lam_re * lam_re + lam_im * lam_im
    coef_re = (n_re * lam_re + n_im * lam_im) / den
    coef_im = (n_im * lam_re - n_re * lam_im) / den
    br, bi = b_re.astype(F32), b_im.astype(F32)
    bbar_re = coef_re[..., None] * br - coef_im[..., None] * bi
    bbar_im = coef_re[..., None] * bi + coef_im[..., None] * br
    eye = jnp.eye(GPB, dtype=F32)

    def in_block(b):
        b = b.reshape(KB, GPB, SSM_STATE, SSM_GROUP)
        return jnp.einsum('kgpc,gh->kgchp', b, eye).reshape(KB, LANES, SB)

    def out_block(c):
        c = c.reshape(KB, GPB, SSM_GROUP, SSM_STATE)
        return jnp.einsum('kgcp,gh->kgphc', c, eye).reshape(KB, SB, LANES)

    bw = jnp.concatenate([in_block(bbar_re), in_block(bbar_im)], axis=2).astype(BF16)
    cw = jnp.concatenate([out_block(c_re.astype(F32)), -out_block(c_im.astype(F32))], axis=1).astype(BF16)
    qr, qi = abar_re, abar_im
    for _ in range(SEG - 1):
        qr, qi = qr * abar_re - qi * abar_im, qr * abar_im + qi * abar_re
    rep = lambda a: jnp.broadcast_to(a.reshape(KB, 1, SB), (KB, SUBLANES, SB))
    return bw, cw, rep(abar_re), rep(abar_im), rep(qr), rep(qi)


def _ln_silu(acc, g, b):
    mu = jnp.mean(acc, axis=-1, keepdims=True)
    xc = acc - mu
    y = xc * lax.rsqrt(jnp.mean(xc * xc, axis=-1, keepdims=True) + NORM_EPS) * g + b
    return y * jax.nn.sigmoid(y)


def _conv_rows_kernel(z_ref, w_ref, cb_ref, g_ref, b_ref, o_ref, zp_ref, *, n_ctx_tiles):
    i = pl.program_id(0)
    w = w_ref[...]
    cb = cb_ref[...]
    g = g_ref[...]
    b = b_ref[...]

    def run(seq):
        nseq = TM // seq
        stride = seq + 2 * PAD
        total = nseq * stride
        zeros = jnp.zeros((PAD, CONV_WIDTH), F32)
        for a in range(nseq):
            zp_ref[0, stride * a:stride * a + PAD, :] = zeros
            zp_ref[0, stride * a + PAD:stride * a + PAD + seq, :] = z_ref[seq * a:seq * (a + 1), :]
            zp_ref[0, stride * a + PAD + seq:stride * (a + 1), :] = zeros
        for k in range(1, SUBLANES):
            for r0 in range(0, total - SUBLANES, CHUNK):
                rows = min(CHUNK, total - SUBLANES - r0)
                zp_ref[k, r0:r0 + rows, :] = zp_ref[0, r0 + k:r0 + k + rows, :]
        for a in range(nseq):
            for r0 in range(0, seq, CHUNK):
                acc = jnp.zeros((CHUNK, CONV_WIDTH), F32) + cb
                for d in range(CONV_SIZE):
                    off = PAD + d - CONV_HALF
                    start = stride * a + r0 + off // SUBLANES * SUBLANES
                    acc = acc + w[d:d + 1, :] * zp_ref[off % SUBLANES, start:start + CHUNK, :]
                o_ref[seq * a + r0:seq * a + r0 + CHUNK, :] = _ln_silu(acc, g, b).astype(o_ref.dtype)

    @pl.when(i < n_ctx_tiles)
    def _():
        run(TM)

    @pl.when(i >= n_ctx_tiles)
    def _():
        run(GRID_W)


def _conv_rows(z, w, cb, g, b, n_ctx_tiles):
    n = z.shape[0]
    nt = n // TM
    vec = pl.BlockSpec((1, CONV_WIDTH), lambda i: (0, 0))
    return pl.pallas_call(
        functools.partial(_conv_rows_kernel, n_ctx_tiles=n_ctx_tiles),
        grid=(nt,),
        in_specs=[
            pl.BlockSpec((TM, CONV_WIDTH), lambda i: (i, 0)),
            pl.BlockSpec((CONV_SIZE, CONV_WIDTH), lambda i: (0, 0)),
            vec, vec, vec,
        ],
        out_specs=pl.BlockSpec((TM, CONV_WIDTH), lambda i: (i, 0)),
        out_shape=jax.ShapeDtypeStruct((n, CONV_WIDTH), BF16),
        scratch_shapes=[pltpu.VMEM((SUBLANES, TM // GRID_W * (GRID_W + 2 * PAD), CONV_WIDTH), F32)],
        compiler_params=_cparams(("arbitrary",)),
        name="conv_rows",
    )(z, w, cb, g, b)


WB = 16
RB = 16


def _conv_cols_kernel(z_ref, w_ref, cb_ref, g_ref, b_ref, o_ref, zp_ref, *, r_ctx, rows):
    w = w_ref[...]
    cb = cb_ref[...]
    g = g_ref[...]
    b = b_ref[...]
    zeros = jnp.zeros((CONV_HALF, WB, CONV_WIDTH), F32)
    zp_ref[0:CONV_HALF] = zeros
    zp_ref[CONV_HALF + rows:CONV_HALF + rows + CONV_HALF] = zeros
    zp_ref[CONV_HALF:CONV_HALF + rows] = z_ref[r_ctx:r_ctx + rows]
    o_ref[0:r_ctx] = jnp.zeros((r_ctx, WB, CONV_WIDTH), o_ref.dtype)

    def column(h, c):
        hs = slice(SUBLANES * h, SUBLANES * (h + 1))
        ls = slice(LANES * c, LANES * (c + 1))
        wv = [jnp.broadcast_to(w[d:d + 1, ls], (SUBLANES, LANES)) for d in range(CONV_SIZE)]
        bias = jnp.broadcast_to(cb[:, ls], (SUBLANES, LANES))

        def body(blk, carry):
            r0 = blk * RB
            acc = [bias] * RB
            for q in range(RB + CONV_SIZE - 1):
                zq = zp_ref[r0 + q, hs, ls]
                for j in range(RB):
                    d = q - j
                    if 0 <= d < CONV_SIZE:
                        acc[j] = acc[j] + wv[d] * zq
            for j in range(RB):
                zp_ref[r0 + j, hs, ls] = acc[j]
            return carry

        lax.fori_loop(0, rows // RB, body, 0)

    for h in range(WB // SUBLANES):
        for c in range(CONV_WIDTH // LANES):
            column(h, c)

    def norm(blk, carry):
        r0 = blk * SUBLANES
        v = zp_ref[pl.ds(r0, SUBLANES)].reshape(SUBLANES * WB, CONV_WIDTH)
        y = _ln_silu(v, g, b).astype(o_ref.dtype)
        o_ref[pl.ds(r_ctx + r0, SUBLANES)] = y.reshape(SUBLANES, WB, CONV_WIDTH)
        return carry

    lax.fori_loop(0, rows // SUBLANES, norm, 0)


def _conv_cols(z, w, cb, g, b, n_ctx):
    n = z.shape[0]
    r_all = n // GRID_W
    r_ctx = n_ctx // GRID_W
    rows = r_all - r_ctx
    z3 = z.reshape(r_all, GRID_W, CONV_WIDTH)
    vec = pl.BlockSpec((1, CONV_WIDTH), lambda j: (0, 0))
    out = pl.pallas_call(
        functools.partial(_conv_cols_kernel, r_ctx=r_ctx, rows=rows),
        grid=(GRID_W // WB,),
        in_specs=[
            pl.BlockSpec((r_all, WB, CONV_WIDTH), lambda j: (0, j, 0)),
            pl.BlockSpec((CONV_SIZE, CONV_WIDTH), lambda j: (0, 0)),
            vec, vec, vec,
        ],
        out_specs=pl.BlockSpec((r_all, WB, CONV_WIDTH), lambda j: (0, j, 0)),
        out_shape=jax.ShapeDtypeStruct((r_all, GRID_W, CONV_WIDTH), BF16),
        scratch_shapes=[pltpu.VMEM((rows + 2 * CONV_HALF, WB, CONV_WIDTH), F32)],
        compiler_params=_cparams(("arbitrary",)),
        name="conv_cols",
    )(z3, w, cb, g, b)
    return out.reshape(n, CONV_WIDTH)


def _route(lt):
    l = [lt[e:e + 1, :] for e in range(N_EXPERTS)]
    m = l[0]
    for e in range(1, N_EXPERTS):
        m = jnp.maximum(m, l[e])
    ex = [jnp.exp(v - m) for v in l]
    den = ex[0]
    for e in range(1, N_EXPERTS):
        den = den + ex[e]
    p = [v / den for v in ex]
    scores = []
    for gidx in range(N_EXPERT_GROUPS):
        q = p[gidx * EXPERTS_PER_GROUP:(gidx + 1) * EXPERTS_PER_GROUP]
        best = None
        for a in range(EXPERTS_PER_GROUP):
            for c in range(a + 1, EXPERTS_PER_GROUP):
                pair = q[a] + q[c]
                best = pair if best is None else jnp.maximum(best, pair)
        scores.append(best)
    bs = scores[0]
    bg = jnp.zeros_like(bs)
    for gidx in range(1, N_EXPERT_GROUPS):
        better = scores[gidx] > bs
        bs = jnp.where(better, scores[gidx], bs)
        bg = jnp.where(better, float(gidx), bg)
    q = []
    for a in range(EXPERTS_PER_GROUP):
        v = p[a]
        for gidx in range(1, N_EXPERT_GROUPS):
            v = jnp.where(bg == float(gidx), p[gidx * EXPERTS_PER_GROUP + a], v)
        q.append(v)
    p1 = q[0]
    i1 = jnp.zeros_like(p1)
    for a in range(1, EXPERTS_PER_GROUP):
        better = q[a] > p1
        p1 = jnp.where(better, q[a], p1)
        i1 = jnp.where(better, float(a), i1)
    p2 = jnp.full_like(p1, -1.0)
    i2 = jnp.zeros_like(p1)
    for a in range(EXPERTS_PER_GROUP):
        better = (q[a] > p2) & (i1 != float(a))
        p2 = jnp.where(better, q[a], p2)
        i2 = jnp.where(better, float(a), i2)
    tot = p1 + p2
    base = bg * float(EXPERTS_PER_GROUP)
    return base + i1, base + i2, p1 / tot, p2 / tot


def _merge_kernel(xa_ref, xb_ref, u_ref, yf_ref, yb_ref, bp_ref, gate_ref, mod_ref, d_ref, pmt_ref,
                  wav_ref, wag_ref, wb_ref, wo_ref, gf_ref, rwh_ref, rwl_ref, rb_ref,
                  xo_ref, h2_ref, r_ref, c_ref, cnt_ref, wa_bf, wb_bf, wo_bf, *, n_ctx_tiles):
    @pl.when(pl.program_id(0) == 0)
    def _():
        wa_bf[:, :D_MODEL] = wav_ref[...].astype(BF16)
        wa_bf[:, D_MODEL:] = wag_ref[...].astype(BF16)
        wb_bf[...] = wb_ref[...].astype(BF16)
        wo_bf[...] = wo_ref[...].astype(BF16)

    ya = jax.nn.gelu(yf_ref[...] + yb_ref[...] + d_ref[...] * u_ref[...].astype(F32), approximate=True)
    ya = jnp.dot(pmt_ref[...], ya.astype(BF16), preferred_element_type=F32).astype(BF16)
    av = jnp.dot(ya, wa_bf[...], preferred_element_type=F32)
    a_out = av[:, :D_MODEL] * jax.nn.sigmoid(av[:, D_MODEL:])
    b_out = jnp.dot(bp_ref[...], wb_bf[...], preferred_element_type=F32)
    gate = gate_ref[...].astype(F32)
    merged = gate[:, :D_MODEL] * a_out + gate[:, D_MODEL:] * b_out
    mx = jnp.dot(merged.astype(BF16), wo_bf[...], preferred_element_type=F32)
    x = jnp.where(pl.program_id(0) < n_ctx_tiles, xa_ref[...], xb_ref[...])
    xn = x + mod_ref[2:3, :] * mx
    xo_ref[...] = xn
    h2 = _norm_mod(xn, gf_ref[...], mod_ref[3:4, :], mod_ref[4:5, :])
    h2_ref[...] = h2
    h_hi, h_lo = _split_bf16(h2)
    nt_dims = (((1,), (1,)), ((), ()))
    lt = lax.dot_general(rwh_ref[...], h_hi, nt_dims, preferred_element_type=F32)
    lt += lax.dot_general(rwh_ref[...], h_lo, nt_dims, preferred_element_type=F32)
    lt += lax.dot_general(rwl_ref[...], h_hi, nt_dims, preferred_element_type=F32)
    lt = lt + rb_ref[...]
    e1, e2, w1, w2 = _route(lt)
    @pl.when(pl.program_id(0) == 0)
    def _():
        cnt_ref[...] = jnp.zeros_like(cnt_ref)

    eid = lax.broadcasted_iota(jnp.int32, (N_EXPERTS, TM), 0).astype(F32)
    hit1 = eid == e1
    hit2 = eid == e2
    onehot = jnp.where(hit1 | hit2, 1.0, 0.0)
    before = lax.broadcasted_iota(jnp.int32, (TM, TM), 0) < lax.broadcasted_iota(jnp.int32, (TM, TM), 1)
    excl = jnp.dot(onehot.astype(BF16), jnp.where(before, 1.0, 0.0).astype(BF16),
                   preferred_element_type=F32)
    seen = excl + cnt_ref[:, 0:1]
    r1 = jnp.sum(jnp.where(hit1, seen, 0.0), axis=0, keepdims=True)
    r2 = jnp.sum(jnp.where(hit2, seen, 0.0), axis=0, keepdims=True)
    cnt_ref[...] = cnt_ref[...] + jnp.sum(onehot, axis=1, keepdims=True)
    c_ref[...] = cnt_ref[...]
    zero = jnp.zeros_like(e1)
    r_ref[...] = jnp.concatenate([e1, e2, w1, w2, r1, r2, zero, zero], axis=0)


def _merge(rows, n, u, yf, yb, bp, gates, mod, d_skip, pmt, w_a_val, w_a_gate, w_b_out, w_out, layer,
           gf, rwh, rwl, rb, n_ctx_tiles):
    nt = n // TM
    kind = lambda i: (jnp.where(i < n_ctx_tiles, 0, 1), 0, 0)
    tile = lambda w: pl.BlockSpec((TM, w), lambda i: (i, 0))
    full = lambda a: pl.BlockSpec(a.shape, lambda i: (0,) * a.ndim)
    lay = lambda a: pl.BlockSpec((None,) + a.shape[1:], lambda i: (layer, 0, 0))
    return pl.pallas_call(
        functools.partial(_merge_kernel, n_ctx_tiles=n_ctx_tiles),
        grid=(nt,),
        in_specs=[
            *rows[1], tile(SSM_WIDTH), tile(SSM_WIDTH), tile(SSM_WIDTH), tile(CONV_WIDTH),
            tile(2 * D_MODEL),
            pl.BlockSpec((None, N_MOD, D_MODEL), kind),
            full(d_skip), full(pmt), lay(w_a_val), lay(w_a_gate), lay(w_b_out), lay(w_out), full(gf),
            full(rwh), full(rwl), full(rb),
        ],
        out_specs=[
            tile(D_MODEL), tile(D_MODEL),
            pl.BlockSpec((SUBLANES, TM), lambda i: (0, i)),
            pl.BlockSpec((N_EXPERTS, LANES), lambda i: (0, 0)),
        ],
        out_shape=[
            jax.ShapeDtypeStruct((n, D_MODEL), F32),
            jax.ShapeDtypeStruct((n, D_MODEL), F32),
            jax.ShapeDtypeStruct((SUBLANES, n), F32),
            jax.ShapeDtypeStruct((N_EXPERTS, LANES), F32),
        ],
        scratch_shapes=[
            pltpu.VMEM((N_EXPERTS, LANES), F32),
            pltpu.VMEM((SSM_WIDTH, 2 * D_MODEL), BF16),
            pltpu.VMEM((CONV_WIDTH, D_MODEL), BF16),
            pltpu.VMEM((D_MODEL, D_MODEL), BF16),
        ],
        compiler_params=_cparams(("arbitrary",)),
        name="merge",
    )(*rows[0], u, yf, yb, bp, gates, mod, d_skip, pmt, w_a_val, w_a_gate, w_b_out, w_out, gf, rwh, rwl, rb)


TMM = 256


def _row_copy(src, i, dst, j, sem):
    return pltpu.make_async_copy(src.at[pl.ds(i, 1)], dst.at[pl.ds(j, 1)], sem)


def _invert_kernel(s1_ref, s2_ref, pad_lo_ref, pad_hi_ref, inv_ref, *, n):
    def clear(j, carry):
        inv_ref[j] = 0
        return carry

    for e in range(N_EXPERTS):
        lax.fori_loop(pad_lo_ref[e], pad_hi_ref[e], clear, 0)
    lax.fori_loop(pad_hi_ref[N_EXPERTS - 1], inv_ref.shape[0], clear, 0)

    def put(t, carry):
        inv_ref[s1_ref[t]] = t
        inv_ref[s2_ref[t]] = t
        return carry

    lax.fori_loop(0, n, put, 0, unroll=8)


def _invert(s1, s2, pad_lo, pad_hi, ns):
    return pl.pallas_call(
        functools.partial(_invert_kernel, n=s1.shape[0]),
        grid_spec=pltpu.PrefetchScalarGridSpec(
            num_scalar_prefetch=4,
            grid=(1,),
            in_specs=[],
            out_specs=pl.BlockSpec(memory_space=pltpu.SMEM),
        ),
        out_shape=jax.ShapeDtypeStruct((ns,), jnp.int32),
        compiler_params=_cparams(("arbitrary",)),
        name="moe_invert",
    )(s1, s2, pad_lo, pad_hi)


def _ffn_kernel(te_ref, nu_ref, inv_ref, h_hbm, wg_ref, wu_ref, wd_ref, y_ref, xbuf0, xbuf1, sem):
    i = pl.program_id(0)
    nu = nu_ref[0]
    xbuf = (xbuf0, xbuf1)

    def issue(tile, slot):
        for r in range(TMM):
            _row_copy(h_hbm, inv_ref[tile * TMM + r], xbuf[slot], r, sem.at[slot]).start(priority=r % 2)

    def drain(slot):
        for r in range(TMM):
            _row_copy(h_hbm, 0, xbuf[slot], 0, sem.at[slot]).wait()

    @pl.when((i == 0) & (nu > 0))
    def _():
        issue(0, 0)

    def step(slot):
        drain(slot)
        issue(jnp.minimum(i + 1, nu - 1), 1 - slot)
        x = xbuf[slot][...].astype(BF16)
        hid = jnp.dot(x, wg_ref[...].astype(BF16), preferred_element_type=F32)
        hid = hid * jax.nn.sigmoid(hid) * jnp.dot(x, wu_ref[...].astype(BF16), preferred_element_type=F32)
        y_ref[...] = jnp.dot(hid.astype(BF16), wd_ref[...].astype(BF16), preferred_element_type=F32)

        @pl.when(i == nu - 1)
        def _():
            drain(1 - slot)

    for slot in range(2):
        @pl.when((i % 2 == slot) & (i < nu))
        def _():
            step(slot)

    @pl.when(i >= nu)
    def _():
        y_ref[...] = jnp.zeros_like(y_ref)


def _ffn(h2, inv, tile_expert, n_used, wg, wu, wd, layer):
    ns = inv.shape[0]
    wspec = lambda *shape: pl.BlockSpec((None, None) + shape, lambda i, te, nu, iv: (layer, te[i], 0, 0))
    return pl.pallas_call(
        _ffn_kernel,
        grid_spec=pltpu.PrefetchScalarGridSpec(
            num_scalar_prefetch=3,
            grid=(ns // TMM,),
            in_specs=[
                pl.BlockSpec(memory_space=pl.ANY),
                wspec(D_MODEL, D_EXPERT), wspec(D_MODEL, D_EXPERT), wspec(D_EXPERT, D_MODEL),
            ],
            out_specs=pl.BlockSpec((TMM, D_MODEL), lambda i, te, nu, iv: (i, 0)),
            scratch_shapes=[pltpu.VMEM((TMM, D_MODEL), F32), pltpu.VMEM((TMM, D_MODEL), F32),
                            pltpu.SemaphoreType.DMA((2,))],
        ),
        out_shape=jax.ShapeDtypeStruct((ns, D_MODEL), F32),
        compiler_params=_cparams(("arbitrary",)),
        name="moe_ffn",
    )(tile_expert, n_used, inv, h2, wg, wu, wd)


def _combine_kernel(s1_ref, s2_ref, x_ref, rt_ref, mod_ref, g_ref, ys_hbm, o_ref, buf, sem, *,
                    first_tile, final):
    i = pl.program_id(0)
    nt = pl.num_programs(0)

    def issue(tile, slot):
        for r in range(TM):
            t = (first_tile + tile) * TM + r
            _row_copy(ys_hbm, s1_ref[t], buf.at[slot], r, sem.at[slot]).start(priority=0)
            _row_copy(ys_hbm, s2_ref[t], buf.at[slot], TM + r, sem.at[slot]).start(priority=1)

    def drain(slot):
        for r in range(2 * TM):
            _row_copy(ys_hbm, 0, buf.at[slot], 0, sem.at[slot]).wait()

    @pl.when(i == 0)
    def _():
        issue(0, 0)

    def step(slot):
        drain(slot)
        issue(jnp.minimum(i + 1, nt - 1), 1 - slot)
        rt = rt_ref[...]
        y = rt[:, 2:3] * buf[slot, 0:TM, :] + rt[:, 3:4] * buf[slot, TM:2 * TM, :]
        out = x_ref[...] + mod_ref[5:6, :] * y
        if final:
            out = out * lax.rsqrt(jnp.mean(out * out, axis=-1, keepdims=True) + NORM_EPS) * g_ref[...]
        o_ref[...] = out

        @pl.when(i == nt - 1)
        def _():
            drain(1 - slot)

    for slot in range(2):
        @pl.when(i % 2 == slot)
        def _():
            step(slot)


def _combine(xn, rt, mod, g, ys, s1, s2, n_ctx_tiles, final):
    n = xn.shape[0]
    first = n_ctx_tiles if final else 0
    tiles = n // TM - first
    kind = lambda i, a, b: (jnp.where(i + first < n_ctx_tiles, 0, 1), 0, 0)
    return pl.pallas_call(
        functools.partial(_combine_kernel, first_tile=first, final=final),
        grid_spec=pltpu.PrefetchScalarGridSpec(
            num_scalar_prefetch=2,
            grid=(tiles,),
            in_specs=[
                pl.BlockSpec((TM, D_MODEL), lambda i, a, b: (i + first, 0)),
                pl.BlockSpec((TM, SUBLANES), lambda i, a, b: (i + first, 0)),
                pl.BlockSpec((None, N_MOD, D_MODEL), kind),
                pl.BlockSpec((1, D_MODEL), lambda i, a, b: (0, 0)),
                pl.BlockSpec(memory_space=pl.ANY),
            ],
            out_specs=pl.BlockSpec((TM, D_MODEL), lambda i, a, b: (i, 0)),
            scratch_shapes=[
                pltpu.VMEM((2, 2 * TM, D_MODEL), F32),
                pltpu.SemaphoreType.DMA((2,)),
            ],
        ),
        out_shape=jax.ShapeDtypeStruct((tiles * TM, D_MODEL), F32),
        compiler_params=_cparams(("arbitrary",)),
        name="moe_combine",
    )(s1, s2, xn, rt, mod, g, ys)


def _moe(xn, h2, r, counts, mod, wg, wu, wd, layer, final_g, n_ctx_tiles, final):
    n = xn.shape[0]
    ns = 2 * n + N_EXPERTS * TMM
    cnt = counts[:, 0].astype(jnp.int32)
    padded = (cnt + TMM - 1) // TMM * TMM
    ends = jnp.cumsum(padded)
    off = ends - padded
    experts = jnp.arange(N_EXPERTS, dtype=jnp.int32)

    def slot(e_row, rank_row):
        hit = e_row.astype(jnp.int32)[:, None] == experts[None, :]
        return jnp.sum(jnp.where(hit, off[None, :], 0), axis=1) + rank_row.astype(jnp.int32)

    s1 = slot(r[0], r[4])
    s2 = slot(r[1], r[5])
    tile_start = jnp.arange(ns // TMM, dtype=jnp.int32) * TMM
    tile_expert = jnp.minimum(jnp.sum(ends[None, :] <= tile_start[:, None], axis=1),
                              N_EXPERTS - 1).astype(jnp.int32)
    n_used = (ends[-1:] // TMM).astype(jnp.int32)
    inv = _invert(s1, s2, off + cnt, ends, ns)
    ys = _ffn(h2, inv, tile_expert, n_used, wg, wu, wd, layer)
    return _combine(xn, r.T, mod, final_g, ys, s1, s2, n_ctx_tiles, final)


def kernel(x, c, ctx, c_ctx, ada_w, ada_b, norm_mix_g, norm_ffn_g, w_in, ssm_a_re, ssm_a_im, ssm_log_dt, ssm_b_re, ssm_b_im, ssm_c_re, ssm_c_im, ssm_d, w_a_val, w_a_gate, conv_w, conv_b, conv_ln_g, conv_ln_b, w_b_out, w_out, router_w, router_b, w_exp_gate, w_exp_up, w_exp_down, final_norm_g):
    depth = ada_w.shape[0]
    bsz, seq, _ = x.shape
    n_ctx = ctx.shape[1]
    assert bsz == 1 and n_ctx == TM and seq % TM == 0 and seq % GRID_W == 0
    n_ctx_tiles = n_ctx // TM
    n = n_ctx + seq

    rows = _row_sources(ctx[0], x[0], 0, n_ctx_tiles)
    cc = jnp.zeros((SUBLANES, D_MODEL), F32).at[0].set(c_ctx).at[1].set(c[0])
    mods = _mods(cc, ada_w, ada_b)[:, :2].reshape(depth, 2, N_MOD, D_MODEL)

    rwt = router_w.T
    rwh, rwl = _split_bf16(rwt)
    rb = router_b.reshape(N_EXPERTS, 1)
    pm, pmt = _ssm_perm()

    for l in range(depth):
        last = l == depth - 1
        u, z, gates = _pre(rows, n, mods[l], norm_mix_g[l].reshape(1, -1), w_in, l, pm, n_ctx_tiles)
        prm = [_ssm_params(ssm_a_re[l, k], ssm_a_im[l, k], ssm_log_dt[l, k], ssm_b_re[l, k],
                           ssm_b_im[l, k], ssm_c_re[l, k], ssm_c_im[l, k]) for k in range(2)]
        ys = _ssm(u, *[jnp.stack(pair) for pair in zip(*prm)], n_ctx_tiles)
        vec = lambda a: a.reshape(1, -1)
        conv_args = (conv_w[l], vec(conv_b[l]), vec(conv_ln_g[l]), vec(conv_ln_b[l]))
        if l % 2 == 0:
            bp = _conv_rows(z, *conv_args, n_ctx_tiles)
        else:
            assert last, "column conv over the context prefix is not needed by the last layer only"
            bp = _conv_cols(z, *conv_args, n_ctx)
        xn, h2, r, counts = _merge(rows, n, u, ys[0], ys[1], bp, gates, mods[l], vec(ssm_d[l]), pmt,
                                   w_a_val, w_a_gate, w_b_out, w_out, l, vec(norm_ffn_g[l]),
                                   rwh, rwl, rb, n_ctx_tiles)
        xc = _moe(xn, h2, r, counts, mods[l], w_exp_gate, w_exp_up, w_exp_down, l,
                  vec(final_norm_g), n_ctx_tiles, final=last)
        rows = _row_sources(xc, xc, n_ctx_tiles, n_ctx_tiles)
    return xc[None]
```

```python
import functools
import math

import jax
import jax.numpy as jnp
from jax import lax
from jax.experimental import pallas as pl
from jax.experimental.pallas import tpu as pltpu

F32 = jnp.float32
BF16 = jnp.bfloat16

D_MODEL = 1024
GRID_W = 64
SSM_WIDTH = 512
SSM_GROUP = 16
SSM_GROUPS = SSM_WIDTH // SSM_GROUP
SSM_STATE = 64
CONV_WIDTH = 512
CONV_SIZE = 31
CONV_HALF = CONV_SIZE // 2
IN_COLS = SSM_WIDTH + 2 * CONV_WIDTH + 2 * D_MODEL
N_EXPERTS = 16
N_EXPERT_GROUPS = 4
EXPERTS_PER_GROUP = N_EXPERTS // N_EXPERT_GROUPS
D_EXPERT = 512
N_MOD = 6
NORM_EPS = 1e-6

TM = 256
LANES = 128
SUBLANES = 8
SEG = TM // SUBLANES
KB = SSM_WIDTH // LANES
GPB = LANES // SSM_GROUP
SB = GPB * SSM_STATE
PAD = 16
CHUNK = 32
VMEM_LIMIT = 56 * 1024 * 1024


def _cparams(sem):
    return pltpu.CompilerParams(dimension_semantics=sem, vmem_limit_bytes=VMEM_LIMIT)


def _split_bf16(a):
    hi = a.astype(BF16)
    lo = (a - hi.astype(F32)).astype(BF16)
    return hi, lo


def _mods_kernel(c_ref, w_ref, b_ref, o_ref):
    s = c_ref[...]
    s = s * jax.nn.sigmoid(s)
    s_hi, s_lo = _split_bf16(s)
    w_hi, w_lo = _split_bf16(w_ref[...])
    acc = jnp.dot(s_hi, w_hi, preferred_element_type=F32)
    acc += jnp.dot(s_lo, w_hi, preferred_element_type=F32)
    acc += jnp.dot(s_hi, w_lo, preferred_element_type=F32)
    o_ref[...] = acc + b_ref[...]


def _mods(cc, ada_w, ada_b):
    depth = ada_w.shape[0]
    cols = ada_w.shape[2]
    cb = 1536
    return pl.pallas_call(
        _mods_kernel,
        grid=(depth, cols // cb),
        in_specs=[
            pl.BlockSpec((SUBLANES, D_MODEL), lambda l, j: (0, 0)),
            pl.BlockSpec((None, D_MODEL, cb), lambda l, j: (l, 0, j)),
            pl.BlockSpec((None, 1, cb), lambda l, j: (l, 0, j)),
        ],
        out_specs=pl.BlockSpec((None, SUBLANES, cb), lambda l, j: (l, 0, j)),
        out_shape=jax.ShapeDtypeStruct((depth, SUBLANES, cols), F32),
        compiler_params=_cparams(("arbitrary", "arbitrary")),
        name="mods",
    )(cc, ada_w, ada_b.reshape(depth, 1, cols))


def _norm_mod(x, g, shift, scale):
    y = x * lax.rsqrt(jnp.mean(x * x, axis=-1, keepdims=True) + NORM_EPS) * g
    return y * (1.0 + scale) + shift


def _pre_kernel(xa_ref, xb_ref, mod_ref, g_ref, w_ref, pm_ref, u_ref, z_ref, gate_ref, wbf_ref, *,
                n_ctx_tiles):
    @pl.when(pl.program_id(0) == 0)
    def _():
        wbf_ref[...] = w_ref[...].astype(BF16)

    x = jnp.where(pl.program_id(0) < n_ctx_tiles, xa_ref[...], xb_ref[...])
    h = _norm_mod(x, g_ref[...], mod_ref[0:1, :], mod_ref[1:2, :])
    p = jnp.dot(h.astype(BF16), wbf_ref[...], preferred_element_type=F32)
    u_ref[...] = jnp.dot(pm_ref[...], p[:, :SSM_WIDTH].astype(BF16),
                         preferred_element_type=F32).astype(BF16)
    v1 = p[:, SSM_WIDTH:SSM_WIDTH + CONV_WIDTH]
    v2 = p[:, SSM_WIDTH + CONV_WIDTH:SSM_WIDTH + 2 * CONV_WIDTH]
    z_ref[...] = v1 * jax.nn.sigmoid(v2)
    gate_ref[...] = jax.nn.sigmoid(p[:, SSM_WIDTH + 2 * CONV_WIDTH:]).astype(BF16)


def _row_sources(rows_ctx, rows_lat, lat_offset, n_ctx_tiles):
    spec_ctx = pl.BlockSpec((TM, D_MODEL), lambda i, *_: (jnp.minimum(i, n_ctx_tiles - 1), 0))
    spec_lat = pl.BlockSpec((TM, D_MODEL),
                            lambda i, *_: (jnp.maximum(i - n_ctx_tiles, 0) + lat_offset, 0))
    return (rows_ctx, rows_lat), (spec_ctx, spec_lat)


def _pre(rows, n, mod, g, w_in, layer, pm, n_ctx_tiles):
    nt = n // TM
    kind = lambda i: (jnp.where(i < n_ctx_tiles, 0, 1), 0, 0)
    return pl.pallas_call(
        functools.partial(_pre_kernel, n_ctx_tiles=n_ctx_tiles),
        grid=(nt,),
        in_specs=[
            *rows[1],
            pl.BlockSpec((None, N_MOD, D_MODEL), kind),
            pl.BlockSpec((1, D_MODEL), lambda i: (0, 0)),
            pl.BlockSpec((None, D_MODEL, IN_COLS), lambda i: (layer, 0, 0), pipeline_mode=pl.Buffered(1)),
            pl.BlockSpec((TM, TM), lambda i: (0, 0)),
        ],
        out_specs=[
            pl.BlockSpec((TM, SSM_WIDTH), lambda i: (i, 0)),
            pl.BlockSpec((TM, CONV_WIDTH), lambda i: (i, 0)),
            pl.BlockSpec((TM, 2 * D_MODEL), lambda i: (i, 0)),
        ],
        out_shape=[
            jax.ShapeDtypeStruct((n, SSM_WIDTH), BF16),
            jax.ShapeDtypeStruct((n, CONV_WIDTH), F32),
            jax.ShapeDtypeStruct((n, 2 * D_MODEL), BF16),
        ],
        scratch_shapes=[pltpu.VMEM((D_MODEL, IN_COLS), BF16)],
        compiler_params=_cparams(("arbitrary",)),
        name="pre",
    )(*rows[0], mod, g, w_in, pm)


def _ssm_kernel(uf_ref, ub_ref, bw_ref, cw_ref, ar_ref, ai_ref, qr_ref, qi_ref,
                yf_ref, yb_ref, cr_ref, ci_ref, sf_ref, sb_ref, hf_ref, hb_ref):
    @pl.when(pl.program_id(0) == 0)
    def _():
        cr_ref[...] = jnp.zeros_like(cr_ref)
        ci_ref[...] = jnp.zeros_like(ci_ref)

    dirs = (0, 1)
    u_refs = (uf_ref, ub_ref)
    y_refs = (yf_ref, yb_ref)
    s_refs = (sf_ref, sb_ref)
    h_refs = (hf_ref, hb_ref)

    def group(d, i):
        return SEG - 1 - i if d == 1 else i

    def block(k):
        lanes = slice(LANES * k, LANES * (k + 1))
        for d in dirs:
            s_refs[d][...] = jnp.dot(u_refs[d][:, lanes], bw_ref[d, k], preferred_element_type=F32)

        ar = [ar_ref[d, k] for d in dirs]
        ai = [ai_ref[d, k] for d in dirs]

        def advance(carry, i):
            out = []
            for d in dirs:
                rows = pl.ds(pl.multiple_of(group(d, i) * SUBLANES, SUBLANES), SUBLANES)
                hr_d, hi_d = carry[2 * d], carry[2 * d + 1]
                out.append(ar[d] * hr_d - ai[d] * hi_d + s_refs[d][rows, 0:SB])
                out.append(ar[d] * hi_d + ai[d] * hr_d + s_refs[d][rows, SB:2 * SB])
            return tuple(out)

        zero = jnp.zeros((SUBLANES, SB), F32)
        ends = lax.fori_loop(0, SEG, lambda i, c: advance(c, i), (zero,) * 4, unroll=2)
        hr = [ends[0], ends[2]]
        hi = [ends[1], ends[3]]

        hpr = [None, None]
        hpi = [None, None]
        for d in dirs:
            asr = qr_ref[d, k, 0:1, :]
            asi = qi_ref[d, k, 0:1, :]
            pr_ = cr_ref[d, k]
            pi_ = ci_ref[d, k]
            prev_r = [None] * SUBLANES
            prev_i = [None] * SUBLANES
            order = range(SUBLANES - 1, -1, -1) if d == 1 else range(SUBLANES)
            for s in order:
                prev_r[s] = pr_
                prev_i[s] = pi_
                er = hr[d][s:s + 1, :]
                ei = hi[d][s:s + 1, :]
                pr_, pi_ = er + asr * pr_ - asi * pi_, ei + asr * pi_ + asi * pr_
            cr_ref[d, k] = pr_
            ci_ref[d, k] = pi_
            hpr[d] = jnp.concatenate(prev_r, axis=0)
            hpi[d] = jnp.concatenate(prev_i, axis=0)

        def rescan(j, carry):
            first = advance(carry, 2 * j)
            second = advance(first, 2 * j + 1)
            for d in dirs:
                lo, hi_ = (first, second) if d == 0 else (second, first)
                pair = j if d == 0 else SEG // 2 - 1 - j
                rows = pl.ds(pl.multiple_of(pair * 2 * SUBLANES, 2 * SUBLANES), 2 * SUBLANES)
                h_refs[d][rows, 0:SB] = jnp.concatenate([lo[2 * d], hi_[2 * d]], axis=0).astype(BF16)
                h_refs[d][rows, SB:2 * SB] = jnp.concatenate(
                    [lo[2 * d + 1], hi_[2 * d + 1]], axis=0).astype(BF16)
            return second

        lax.fori_loop(0, SEG // 2, rescan, (hpr[0], hpi[0], hpr[1], hpi[1]))

        for d in dirs:
            y_refs[d][:, lanes] = jnp.dot(h_refs[d][...], cw_ref[d, k],
                                          preferred_element_type=F32).astype(BF16)

    for k in range(KB):
        block(k)


def _ssm(u, bw, cw, ar, ai, qr, qi, n_ctx_tiles):
    n = u.shape[0]
    nt = n // TM

    def fwd(t):
        return (t, 0)

    def bwd(t):
        return (jnp.where(t < n_ctx_tiles, n_ctx_tiles - 1 - t, nt - 1 - (t - n_ctx_tiles)), 0)

    full = lambda a: pl.BlockSpec(a.shape, lambda t: (0,) * a.ndim)
    return pl.pallas_call(
        _ssm_kernel,
        grid=(nt,),
        in_specs=[
            pl.BlockSpec((TM, SSM_WIDTH), fwd),
            pl.BlockSpec((TM, SSM_WIDTH), bwd),
            full(bw), full(cw), full(ar), full(ai), full(qr), full(qi),
        ],
        out_specs=[pl.BlockSpec((TM, SSM_WIDTH), fwd), pl.BlockSpec((TM, SSM_WIDTH), bwd)],
        out_shape=[jax.ShapeDtypeStruct((n, SSM_WIDTH), BF16)] * 2,
        scratch_shapes=[
            pltpu.VMEM((2, KB, 1, SB), F32),
            pltpu.VMEM((2, KB, 1, SB), F32),
            pltpu.VMEM((TM, 2 * SB), F32),
            pltpu.VMEM((TM, 2 * SB), F32),
            pltpu.VMEM((TM, 2 * SB), BF16),
            pltpu.VMEM((TM, 2 * SB), BF16),
        ],
        compiler_params=_cparams(("arbitrary",)),
        name="ssm",
    )(u, u, bw, cw, ar, ai, qr, qi)


def _ssm_perm():
    p = jnp.arange(TM)
    src = SEG * (p % SUBLANES) + p // SUBLANES
    pm = (src[:, None] == jnp.arange(TM)[None, :]).astype(BF16)
    return pm, pm.T


def _ssm_params(a_re, a_im, log_dt, b_re, b_im, c_re, c_im):
    lam_re = jnp.minimum(a_re.astype(F32), -1e-4)
    lam_im = a_im.astype(F32)
    dt = jnp.exp(log_dt.astype(F32))[:, None]
    mag = jnp.exp(lam_re * dt)
    abar_re = mag * jnp.cos(lam_im * dt)
    abar_im = mag * jnp.sin(lam_im * dt)
    n_re, n_im = abar_re - 1.0, abar_im
    den = lam_re * lam_re + lam_im * lam_im
    coef_re = (n_re * lam_re + n_im * lam_im) / den
    coef_im = (n_im * lam_re - n_re * lam_im) / den
    br, bi = b_re.astype(F32), b_im.astype(F32)
    bbar_re = coef_re[..., None] * br - coef_im[..., None] * bi
    bbar_im = coef_re[..., None] * bi + coef_im[..., None] * br
    eye = jnp.eye(GPB, dtype=F32)

    def in_block(b):
        b = b.reshape(KB, GPB, SSM_STATE, SSM_GROUP)
        return jnp.einsum('kgpc,gh->kgchp', b, eye).reshape(KB, LANES, SB)

    def out_block(c):
        c = c.reshape(KB, GPB, SSM_GROUP, SSM_STATE)
        return jnp.einsum('kgcp,gh->kgphc', c, eye).reshape(KB, SB, LANES)

    bw = jnp.concatenate([in_block(bbar_re), in_block(bbar_im)], axis=2).astype(BF16)
    cw = jnp.concatenate([out_block(c_re.astype(F32)), -out_block(c_im.astype(F32))], axis=1).astype(BF16)
    qr, qi = abar_re, abar_im
    for _ in range(SEG - 1):
        qr, qi = qr * abar_re - qi * abar_im, qr * abar_im + qi * abar_re
    rep = lambda a: jnp.broadcast_to(a.reshape(KB, 1, SB), (KB, SUBLANES, SB))
    return bw, cw, rep(abar_re), rep(abar_im), rep(qr), rep(qi)


def _ln_silu(acc, g, b):
    mu = jnp.mean(acc, axis=-1, keepdims=True)
    xc = acc - mu
    y = xc * lax.rsqrt(jnp.mean(xc * xc, axis=-1, keepdims=True) + NORM_EPS) * g + b
    return y * jax.nn.sigmoid(y)


def _conv_rows_kernel(z_ref, w_ref, cb_ref, g_ref, b_ref, o_ref, zp_ref, *, n_ctx_tiles):
    i = pl.program_id(0)
    w = w_ref[...]
    cb = cb_ref[...]
    g = g_ref[...]
    b = b_ref[...]

    def run(seq):
        nseq = TM // seq
        stride = seq + 2 * PAD
        total = nseq * stride
        zeros = jnp.zeros((PAD, CONV_WIDTH), F32)
        for a in range(nseq):
            zp_ref[0, stride * a:stride * a + PAD, :] = zeros
            zp_ref[0, stride * a + PAD:stride * a + PAD + seq, :] = z_ref[seq * a:seq * (a + 1), :]
            zp_ref[0, stride * a + PAD + seq:stride * (a + 1), :] = zeros
        for k in range(1, SUBLANES):
            for r0 in range(0, total - SUBLANES, CHUNK):
                rows = min(CHUNK, total - SUBLANES - r0)
                zp_ref[k, r0:r0 + rows, :] = zp_ref[0, r0 + k:r0 + k + rows, :]
        for a in range(nseq):
            for r0 in range(0, seq, CHUNK):
                acc = jnp.zeros((CHUNK, CONV_WIDTH), F32) + cb
                for d in range(CONV_SIZE):
                    off = PAD + d - CONV_HALF
                    start = stride * a + r0 + off // SUBLANES * SUBLANES
                    acc = acc + w[d:d + 1, :] * zp_ref[off % SUBLANES, start:start + CHUNK, :]
                o_ref[seq * a + r0:seq * a + r0 + CHUNK, :] = _ln_silu(acc, g, b).astype(o_ref.dtype)

    @pl.when(i < n_ctx_tiles)
    def _():
        run(TM)

    @pl.when(i >= n_ctx_tiles)
    def _():
        run(GRID_W)


def _conv_rows(z, w, cb, g, b, n_ctx_tiles):
    n = z.shape[0]
    nt = n // TM
    vec = pl.BlockSpec((1, CONV_WIDTH), lambda i: (0, 0))
    return pl.pallas_call(
        functools.partial(_conv_rows_kernel, n_ctx_tiles=n_ctx_tiles),
        grid=(nt,),
        in_specs=[
            pl.BlockSpec((TM, CONV_WIDTH), lambda i: (i, 0)),
            pl.BlockSpec((CONV_SIZE, CONV_WIDTH), lambda i: (0, 0)),
            vec, vec, vec,
        ],
        out_specs=pl.BlockSpec((TM, CONV_WIDTH), lambda i: (i, 0)),
        out_shape=jax.ShapeDtypeStruct((n, CONV_WIDTH), BF16),
        scratch_shapes=[pltpu.VMEM((SUBLANES, TM // GRID_W * (GRID_W + 2 * PAD), CONV_WIDTH), F32)],
        compiler_params=_cparams(("arbitrary",)),
        name="conv_rows",
    )(z, w, cb, g, b)


WB = 16
RB = 16


def _conv_cols_kernel(z_ref, w_ref, cb_ref, g_ref, b_ref, o_ref, zp_ref, *, r_ctx, rows):
    w = w_ref[...]
    cb = cb_ref[...]
    g = g_ref[...]
    b = b_ref[...]
    zeros = jnp.zeros((CONV_HALF, WB, CONV_WIDTH), F32)
    zp_ref[0:CONV_HALF] = zeros
    zp_ref[CONV_HALF + rows:CONV_HALF + rows + CONV_HALF] = zeros
    zp_ref[CONV_HALF:CONV_HALF + rows] = z_ref[r_ctx:r_ctx + rows]
    o_ref[0:r_ctx] = jnp.zeros((r_ctx, WB, CONV_WIDTH), o_ref.dtype)

    def column(h, c):
        hs = slice(SUBLANES * h, SUBLANES * (h + 1))
        ls = slice(LANES * c, LANES * (c + 1))
        wv = [jnp.broadcast_to(w[d:d + 1, ls], (SUBLANES, LANES)) for d in range(CONV_SIZE)]
        bias = jnp.broadcast_to(cb[:, ls], (SUBLANES, LANES))

        def body(blk, carry):
            r0 = blk * RB
            acc = [bias] * RB
            for q in range(RB + CONV_SIZE - 1):
                zq = zp_ref[r0 + q, hs, ls]
                for j in range(RB):
                    d = q - j
                    if 0 <= d < CONV_SIZE:
                        acc[j] = acc[j] + wv[d] * zq
            for j in range(RB):
                zp_ref[r0 + j, hs, ls] = acc[j]
            return carry

        lax.fori_loop(0, rows // RB, body, 0)

    for h in range(WB // SUBLANES):
        for c in range(CONV_WIDTH // LANES):
            column(h, c)

    def norm(blk, carry):
        r0 = blk * SUBLANES
        v = zp_ref[pl.ds(r0, SUBLANES)].reshape(SUBLANES * WB, CONV_WIDTH)
        y = _ln_silu(v, g, b).astype(o_ref.dtype)
        o_ref[pl.ds(r_ctx + r0, SUBLANES)] = y.reshape(SUBLANES, WB, CONV_WIDTH)
        return carry

    lax.fori_loop(0, rows // SUBLANES, norm, 0)


def _conv_cols(z, w, cb, g, b, n_ctx):
    n = z.shape[0]
    r_all = n // GRID_W
    r_ctx = n_ctx // GRID_W
    rows = r_all - r_ctx
    z3 = z.reshape(r_all, GRID_W, CONV_WIDTH)
    vec = pl.BlockSpec((1, CONV_WIDTH), lambda j: (0, 0))
    out = pl.pallas_call(
        functools.partial(_conv_cols_kernel, r_ctx=r_ctx, rows=rows),
        grid=(GRID_W // WB,),
        in_specs=[
            pl.BlockSpec((r_all, WB, CONV_WIDTH), lambda j: (0, j, 0)),
            pl.BlockSpec((CONV_SIZE, CONV_WIDTH), lambda j: (0, 0)),
            vec, vec, vec,
        ],
        out_specs=pl.BlockSpec((r_all, WB, CONV_WIDTH), lambda j: (0, j, 0)),
        out_shape=jax.ShapeDtypeStruct((r_all, GRID_W, CONV_WIDTH), BF16),
        scratch_shapes=[pltpu.VMEM((rows + 2 * CONV_HALF, WB, CONV_WIDTH), F32)],
        compiler_params=_cparams(("arbitrary",)),
        name="conv_cols",
    )(z3, w, cb, g, b)
    return out.reshape(n, CONV_WIDTH)


def _route(lt):
    l = [lt[e:e + 1, :] for e in range(N_EXPERTS)]
    m = l[0]
    for e in range(1, N_EXPERTS):
        m = jnp.maximum(m, l[e])
    ex = [jnp.exp(v - m) for v in l]
    den = ex[0]
    for e in range(1, N_EXPERTS):
        den = den + ex[e]
    p = [v / den for v in ex]
    scores = []
    for gidx in range(N_EXPERT_GROUPS):
        q = p[gidx * EXPERTS_PER_GROUP:(gidx + 1) * EXPERTS_PER_GROUP]
        best = None
        for a in range(EXPERTS_PER_GROUP):
            for c in range(a + 1, EXPERTS_PER_GROUP):
                pair = q[a] + q[c]
                best = pair if best is None else jnp.maximum(best, pair)
        scores.append(best)
    bs = scores[0]
    bg = jnp.zeros_like(bs)
    for gidx in range(1, N_EXPERT_GROUPS):
        better = scores[gidx] > bs
        bs = jnp.where(better, scores[gidx], bs)
        bg = jnp.where(better, float(gidx), bg)
    q = []
    for a in range(EXPERTS_PER_GROUP):
        v = p[a]
        for gidx in range(1, N_EXPERT_GROUPS):
            v = jnp.where(bg == float(gidx), p[gidx * EXPERTS_PER_GROUP + a], v)
        q.append(v)
    p1 = q[0]
    i1 = jnp.zeros_like(p1)
    for a in range(1, EXPERTS_PER_GROUP):
        better = q[a] > p1
        p1 = jnp.where(better, q[a], p1)
        i1 = jnp.where(better, float(a), i1)
    p2 = jnp.full_like(p1, -1.0)
    i2 = jnp.zeros_like(p1)
    for a in range(EXPERTS_PER_GROUP):
        better = (q[a] > p2) & (i1 != float(a))
        p2 = jnp.where(better, q[a], p2)
        i2 = jnp.where(better, float(a), i2)
    tot = p1 + p2
    base = bg * float(EXPERTS_PER_GROUP)
    return base + i1, base + i2, p1 / tot, p2 / tot


def _merge_kernel(xa_ref, xb_ref, u_ref, yf_ref, yb_ref, bp_ref, gate_ref, mod_ref, d_ref, pmt_ref,
                  wav_ref, wag_ref, wb_ref, wo_ref, gf_ref, rwh_ref, rwl_ref, rb_ref,
                  xo_ref, h2_ref, r_ref, c_ref, cnt_ref, wa_bf, wb_bf, wo_bf, *, n_ctx_tiles):
    @pl.when(pl.program_id(0) == 0)
    def _():
        wa_bf[:, :D_MODEL] = wav_ref[...].astype(BF16)
        wa_bf[:, D_MODEL:] = wag_ref[...].astype(BF16)
        wb_bf[...] = wb_ref[...].astype(BF16)
        wo_bf[...] = wo_ref[...].astype(BF16)

    ya = jax.nn.gelu(yf_ref[...].astype(F32) + yb_ref[...].astype(F32)
                     + d_ref[...] * u_ref[...].astype(F32), approximate=True)
    ya = jnp.dot(pmt_ref[...], ya.astype(BF16), preferred_element_type=F32).astype(BF16)
    av = jnp.dot(ya, wa_bf[...], preferred_element_type=F32)
    a_out = av[:, :D_MODEL] * jax.nn.sigmoid(av[:, D_MODEL:])
    b_out = jnp.dot(bp_ref[...], wb_bf[...], preferred_element_type=F32)
    gate = gate_ref[...].astype(F32)
    merged = gate[:, :D_MODEL] * a_out + gate[:, D_MODEL:] * b_out
    mx = jnp.dot(merged.astype(BF16), wo_bf[...], preferred_element_type=F32)
    x = jnp.where(pl.program_id(0) < n_ctx_tiles, xa_ref[...], xb_ref[...])
    xn = x + mod_ref[2:3, :] * mx
    xo_ref[...] = xn
    h2 = _norm_mod(xn, gf_ref[...], mod_ref[3:4, :], mod_ref[4:5, :])
    h2_ref[...] = h2
    h_hi, h_lo = _split_bf16(h2)
    nt_dims = (((1,), (1,)), ((), ()))
    lt = lax.dot_general(rwh_ref[...], h_hi, nt_dims, preferred_element_type=F32)
    lt += lax.dot_general(rwh_ref[...], h_lo, nt_dims, preferred_element_type=F32)
    lt += lax.dot_general(rwl_ref[...], h_hi, nt_dims, preferred_element_type=F32)
    lt = lt + rb_ref[...]
    e1, e2, w1, w2 = _route(lt)
    @pl.when(pl.program_id(0) == 0)
    def _():
        cnt_ref[...] = jnp.zeros_like(cnt_ref)

    eid = lax.broadcasted_iota(jnp.int32, (N_EXPERTS, TM), 0).astype(F32)
    hit1 = eid == e1
    hit2 = eid == e2
    onehot = jnp.where(hit1 | hit2, 1.0, 0.0)
    before = lax.broadcasted_iota(jnp.int32, (TM, TM), 0) < lax.broadcasted_iota(jnp.int32, (TM, TM), 1)
    excl = jnp.dot(onehot.astype(BF16), jnp.where(before, 1.0, 0.0).astype(BF16),
                   preferred_element_type=F32)
    seen = excl + cnt_ref[:, 0:1]
    r1 = jnp.sum(jnp.where(hit1, seen, 0.0), axis=0, keepdims=True)
    r2 = jnp.sum(jnp.where(hit2, seen, 0.0), axis=0, keepdims=True)
    cnt_ref[...] = cnt_ref[...] + jnp.sum(onehot, axis=1, keepdims=True)
    c_ref[...] = cnt_ref[...]
    zero = jnp.zeros_like(e1)
    r_ref[...] = jnp.concatenate([e1, e2, w1, w2, r1, r2, zero, zero], axis=0)


def _merge(rows, n, u, yf, yb, bp, gates, mod, d_skip, pmt, w_a_val, w_a_gate, w_b_out, w_out, layer,
           gf, rwh, rwl, rb, n_ctx_tiles):
    nt = n // TM
    kind = lambda i: (jnp.where(i < n_ctx_tiles, 0, 1), 0, 0)
    tile = lambda w: pl.BlockSpec((TM, w), lambda i: (i, 0))
    full = lambda a: pl.BlockSpec(a.shape, lambda i: (0,) * a.ndim)
    lay = lambda a: pl.BlockSpec((None,) + a.shape[1:], lambda i: (layer, 0, 0))
    return pl.pallas_call(
        functools.partial(_merge_kernel, n_ctx_tiles=n_ctx_tiles),
        grid=(nt,),
        in_specs=[
            *rows[1], tile(SSM_WIDTH), tile(SSM_WIDTH), tile(SSM_WIDTH), tile(CONV_WIDTH),
            tile(2 * D_MODEL),
            pl.BlockSpec((None, N_MOD, D_MODEL), kind),
            full(d_skip), full(pmt), lay(w_a_val), lay(w_a_gate), lay(w_b_out), lay(w_out), full(gf),
            full(rwh), full(rwl), full(rb),
        ],
        out_specs=[
            tile(D_MODEL), tile(D_MODEL),
            pl.BlockSpec((SUBLANES, TM), lambda i: (0, i)),
            pl.BlockSpec((N_EXPERTS, LANES), lambda i: (0, 0)),
        ],
        out_shape=[
            jax.ShapeDtypeStruct((n, D_MODEL), F32),
            jax.ShapeDtypeStruct((n, D_MODEL), F32),
            jax.ShapeDtypeStruct((SUBLANES, n), F32),
            jax.ShapeDtypeStruct((N_EXPERTS, LANES), F32),
        ],
        scratch_shapes=[
            pltpu.VMEM((N_EXPERTS, LANES), F32),
            pltpu.VMEM((SSM_WIDTH, 2 * D_MODEL), BF16),
            pltpu.VMEM((CONV_WIDTH, D_MODEL), BF16),
            pltpu.VMEM((D_MODEL, D_MODEL), BF16),
        ],
        compiler_params=_cparams(("arbitrary",)),
        name="merge",
    )(*rows[0], u, yf, yb, bp, gates, mod, d_skip, pmt, w_a_val, w_a_gate, w_b_out, w_out, gf, rwh, rwl, rb)


TMM = 256


def _row_copy(src, i, dst, j, sem):
    return pltpu.make_async_copy(src.at[pl.ds(i, 1)], dst.at[pl.ds(j, 1)], sem)


def _invert_kernel(s1_ref, s2_ref, pad_lo_ref, pad_hi_ref, inv_ref, *, n):
    def clear(j, carry):
        inv_ref[j] = 0
        return carry

    for e in range(N_EXPERTS):
        lax.fori_loop(pad_lo_ref[e], pad_hi_ref[e], clear, 0)
    lax.fori_loop(pad_hi_ref[N_EXPERTS - 1], inv_ref.shape[0], clear, 0)

    def put(t, carry):
        inv_ref[s1_ref[t]] = t
        inv_ref[s2_ref[t]] = t
        return carry

    lax.fori_loop(0, n, put, 0, unroll=8)


def _invert(s1, s2, pad_lo, pad_hi, ns):
    return pl.pallas_call(
        functools.partial(_invert_kernel, n=s1.shape[0]),
        grid_spec=pltpu.PrefetchScalarGridSpec(
            num_scalar_prefetch=4,
            grid=(1,),
            in_specs=[],
            out_specs=pl.BlockSpec(memory_space=pltpu.SMEM),
        ),
        out_shape=jax.ShapeDtypeStruct((ns,), jnp.int32),
        compiler_params=_cparams(("arbitrary",)),
        name="moe_invert",
    )(s1, s2, pad_lo, pad_hi)


def _ffn_kernel(te_ref, nu_ref, inv_ref, h_hbm, wg_ref, wu_ref, wd_ref, y_ref, xbuf0, xbuf1, sem,
                wg_bf, wu_bf, wd_bf):
    i = pl.program_id(0)
    nu = nu_ref[0]
    xbuf = (xbuf0, xbuf1)

    def issue(tile, slot):
        for r in range(TMM):
            _row_copy(h_hbm, inv_ref[tile * TMM + r], xbuf[slot], r, sem.at[slot]).start()

    def drain(slot):
        for r in range(TMM):
            _row_copy(h_hbm, 0, xbuf[slot], 0, sem.at[slot]).wait()

    @pl.when((i == 0) & (nu > 0))
    def _():
        issue(0, 0)

    @pl.when((i < nu) & ((i == 0) | (te_ref[i] != te_ref[jnp.maximum(i - 1, 0)])))
    def _():
        wg_bf[...] = wg_ref[...].astype(BF16)
        wu_bf[...] = wu_ref[...].astype(BF16)
        wd_bf[...] = wd_ref[...].astype(BF16)

    def step(slot):
        drain(slot)
        issue(jnp.minimum(i + 1, nu - 1), 1 - slot)
        x = xbuf[slot][...].astype(BF16)
        hid = jnp.dot(x, wg_bf[...], preferred_element_type=F32)
        hid = hid * jax.nn.sigmoid(hid) * jnp.dot(x, wu_bf[...], preferred_element_type=F32)
        y_ref[...] = jnp.dot(hid.astype(BF16), wd_bf[...], preferred_element_type=F32)

        @pl.when(i == nu - 1)
        def _():
            drain(1 - slot)

    for slot in range(2):
        @pl.when((i % 2 == slot) & (i < nu))
        def _():
            step(slot)

    @pl.when(i >= nu)
    def _():
        y_ref[...] = jnp.zeros_like(y_ref)


def _ffn(h2, inv, tile_expert, n_used, wg, wu, wd, layer):
    ns = inv.shape[0]
    wspec = lambda *shape: pl.BlockSpec((None, None) + shape, lambda i, te, nu, iv: (layer, te[i], 0, 0))
    return pl.pallas_call(
        _ffn_kernel,
        grid_spec=pltpu.PrefetchScalarGridSpec(
            num_scalar_prefetch=3,
            grid=(ns // TMM,),
            in_specs=[
                pl.BlockSpec(memory_space=pl.ANY),
                wspec(D_MODEL, D_EXPERT), wspec(D_MODEL, D_EXPERT), wspec(D_EXPERT, D_MODEL),
            ],
            out_specs=pl.BlockSpec((TMM, D_MODEL), lambda i, te, nu, iv: (i, 0)),
            scratch_shapes=[pltpu.VMEM((TMM, D_MODEL), F32), pltpu.VMEM((TMM, D_MODEL), F32),
                            pltpu.SemaphoreType.DMA((2,)),
                            pltpu.VMEM((D_MODEL, D_EXPERT), BF16), pltpu.VMEM((D_MODEL, D_EXPERT), BF16),
                            pltpu.VMEM((D_EXPERT, D_MODEL), BF16)],
        ),
        out_shape=jax.ShapeDtypeStruct((ns, D_MODEL), F32),
        compiler_params=_cparams(("arbitrary",)),
        name="moe_ffn",
    )(tile_expert, n_used, inv, h2, wg, wu, wd)


def _combine_kernel(s1_ref, s2_ref, x_ref, rt_ref, mod_ref, g_ref, ys_hbm, o_ref, buf, sem, *,
                    first_tile, final):
    i = pl.program_id(0)
    nt = pl.num_programs(0)

    def issue(tile, slot):
        for r in range(TM):
            t = (first_tile + tile) * TM + r
            _row_copy(ys_hbm, s1_ref[t], buf.at[slot], r, sem.at[slot]).start()
            _row_copy(ys_hbm, s2_ref[t], buf.at[slot], TM + r, sem.at[slot]).start()

    def drain(slot):
        for r in range(2 * TM):
            _row_copy(ys_hbm, 0, buf.at[slot], 0, sem.at[slot]).wait()

    @pl.when(i == 0)
    def _():
        issue(0, 0)

    def step(slot):
        drain(slot)
        issue(jnp.minimum(i + 1, nt - 1), 1 - slot)
        rt = rt_ref[...]
        y = rt[:, 2:3] * buf[slot, 0:TM, :] + rt[:, 3:4] * buf[slot, TM:2 * TM, :]
        out = x_ref[...] + mod_ref[5:6, :] * y
        if final:
            out = out * lax.rsqrt(jnp.mean(out * out, axis=-1, keepdims=True) + NORM_EPS) * g_ref[...]
        o_ref[...] = out

        @pl.when(i == nt - 1)
        def _():
            drain(1 - slot)

    for slot in range(2):
        @pl.when(i % 2 == slot)
        def _():
            step(slot)


def _combine(xn, rt, mod, g, ys, s1, s2, n_ctx_tiles, final):
    n = xn.shape[0]
    first = n_ctx_tiles if final else 0
    tiles = n // TM - first
    kind = lambda i, a, b: (jnp.where(i + first < n_ctx_tiles, 0, 1), 0, 0)
    return pl.pallas_call(
        functools.partial(_combine_kernel, first_tile=first, final=final),
        grid_spec=pltpu.PrefetchScalarGridSpec(
            num_scalar_prefetch=2,
            grid=(tiles,),
            in_specs=[
                pl.BlockSpec((TM, D_MODEL), lambda i, a, b: (i + first, 0)),
                pl.BlockSpec((TM, SUBLANES), lambda i, a, b: (i + first, 0)),
                pl.BlockSpec((None, N_MOD, D_MODEL), kind),
                pl.BlockSpec((1, D_MODEL), lambda i, a, b: (0, 0)),
                pl.BlockSpec(memory_space=pl.ANY),
            ],
            out_specs=pl.BlockSpec((TM, D_MODEL), lambda i, a, b: (i, 0)),
            scratch_shapes=[
                pltpu.VMEM((2, 2 * TM, D_MODEL), F32),
                pltpu.SemaphoreType.DMA((2,)),
            ],
        ),
        out_shape=jax.ShapeDtypeStruct((tiles * TM, D_MODEL), F32),
        compiler_params=_cparams(("arbitrary",)),
        name="moe_combine",
    )(s1, s2, xn, rt, mod, g, ys)


def _moe(xn, h2, r, counts, mod, wg, wu, wd, layer, final_g, n_ctx_tiles, final):
    n = xn.shape[0]
    ns = 2 * n + N_EXPERTS * TMM
    cnt = counts[:, 0].astype(jnp.int32)
    padded = (cnt + TMM - 1) // TMM * TMM
    ends = jnp.cumsum(padded)
    off = ends - padded
    experts = jnp.arange(N_EXPERTS, dtype=jnp.int32)

    def slot(e_row, rank_row):
        hit = e_row.astype(jnp.int32)[:, None] == experts[None, :]
        return jnp.sum(jnp.where(hit, off[None, :], 0), axis=1) + rank_row.astype(jnp.int32)

    s1 = slot(r[0], r[4])
    s2 = slot(r[1], r[5])
    tile_start = jnp.arange(ns // TMM, dtype=jnp.int32) * TMM
    tile_expert = jnp.minimum(jnp.sum(ends[None, :] <= tile_start[:, None], axis=1),
                              N_EXPERTS - 1).astype(jnp.int32)
    n_used = (ends[-1:] // TMM).astype(jnp.int32)
    inv = _invert(s1, s2, off + cnt, ends, ns)
    ys = _ffn(h2, inv, tile_expert, n_used, wg, wu, wd, layer)
    return _combine(xn, r.T, mod, final_g, ys, s1, s2, n_ctx_tiles, final)


def kernel(x, c, ctx, c_ctx, ada_w, ada_b, norm_mix_g, norm_ffn_g, w_in, ssm_a_re, ssm_a_im, ssm_log_dt, ssm_b_re, ssm_b_im, ssm_c_re, ssm_c_im, ssm_d, w_a_val, w_a_gate, conv_w, conv_b, conv_ln_g, conv_ln_b, w_b_out, w_out, router_w, router_b, w_exp_gate, w_exp_up, w_exp_down, final_norm_g):
    depth = ada_w.shape[0]
    bsz, seq, _ = x.shape
    n_ctx = ctx.shape[1]
    assert bsz == 1 and n_ctx == TM and seq % TM == 0 and seq % GRID_W == 0
    n_ctx_tiles = n_ctx // TM
    n = n_ctx + seq

    rows = _row_sources(ctx[0], x[0], 0, n_ctx_tiles)
    cc = jnp.zeros((SUBLANES, D_MODEL), F32).at[0].set(c_ctx).at[1].set(c[0])
    mods = _mods(cc, ada_w, ada_b)[:, :2].reshape(depth, 2, N_MOD, D_MODEL)

    rwt = router_w.T
    rwh, rwl = _split_bf16(rwt)
    rb = router_b.reshape(N_EXPERTS, 1)
    pm, pmt = _ssm_perm()

    for l in range(depth):
        last = l == depth - 1
        u, z, gates = _pre(rows, n, mods[l], norm_mix_g[l].reshape(1, -1), w_in, l, pm, n_ctx_tiles)
        prm = [_ssm_params(ssm_a_re[l, k], ssm_a_im[l, k], ssm_log_dt[l, k], ssm_b_re[l, k],
                           ssm_b_im[l, k], ssm_c_re[l, k], ssm_c_im[l, k]) for k in range(2)]
        ys = _ssm(u, *[jnp.stack(pair) for pair in zip(*prm)], n_ctx_tiles)
        vec = lambda a: a.reshape(1, -1)
        conv_args = (conv_w[l], vec(conv_b[l]), vec(conv_ln_g[l]), vec(conv_ln_b[l]))
        if l % 2 == 0:
            bp = _conv_rows(z, *conv_args, n_ctx_tiles)
        else:
            assert last, "column conv over the context prefix is not needed by the last layer only"
            bp = _conv_cols(z, *conv_args, n_ctx)
        xn, h2, r, counts = _merge(rows, n, u, ys[0], ys[1], bp, gates, mods[l], vec(ssm_d[l]), pmt,
                                   w_a_val, w_a_gate, w_b_out, w_out, l, vec(norm_ffn_g[l]),
                                   rwh, rwl, rb, n_ctx_tiles)
        xc = _moe(xn, h2, r, counts, mods[l], w_exp_gate, w_exp_up, w_exp_down, l,
                  vec(final_norm_g), n_ctx_tiles, final=last)
        rows = _row_sources(xc, xc, n_ctx_tiles, n_ctx_tiles)
    return xc[None]
```

```python
import functools
import math

import jax
import jax.numpy as jnp
from jax import lax
from jax.experimental import pallas as pl
from jax.experimental.pallas import tpu as pltpu

F32 = jnp.float32
BF16 = jnp.bfloat16

D_MODEL = 1024
GRID_W = 64
SSM_WIDTH = 512
SSM_GROUP = 16
SSM_GROUPS = SSM_WIDTH // SSM_GROUP
SSM_STATE = 64
CONV_WIDTH = 512
CONV_SIZE = 31
CONV_HALF = CONV_SIZE // 2
IN_COLS = SSM_WIDTH + 2 * CONV_WIDTH + 2 * D_MODEL
N_EXPERTS = 16
N_EXPERT_GROUPS = 4
EXPERTS_PER_GROUP = N_EXPERTS // N_EXPERT_GROUPS
D_EXPERT = 512
N_MOD = 6
NORM_EPS = 1e-6

TM = 256
LANES = 128
SUBLANES = 8
SEG = TM // SUBLANES
KB = SSM_WIDTH // LANES
GPB = LANES // SSM_GROUP
SB = GPB * SSM_STATE
PAD = 16
CHUNK = 32
VMEM_LIMIT = 56 * 1024 * 1024


def _cparams(sem):
    return pltpu.CompilerParams(dimension_semantics=sem, vmem_limit_bytes=VMEM_LIMIT)


def _split_bf16(a):
    hi = a.astype(BF16)
    lo = (a - hi.astype(F32)).astype(BF16)
    return hi, lo


def _mods_kernel(c_ref, w_ref, b_ref, o_ref):
    s = c_ref[...]
    s = s * jax.nn.sigmoid(s)
    s_hi, s_lo = _split_bf16(s)
    w_hi, w_lo = _split_bf16(w_ref[...])
    acc = jnp.dot(s_hi, w_hi, preferred_element_type=F32)
    acc += jnp.dot(s_lo, w_hi, preferred_element_type=F32)
    acc += jnp.dot(s_hi, w_lo, preferred_element_type=F32)
    o_ref[...] = acc + b_ref[...]


def _mods(cc, ada_w, ada_b):
    depth = ada_w.shape[0]
    cols = ada_w.shape[2]
    cb = 1536
    return pl.pallas_call(
        _mods_kernel,
        grid=(depth, cols // cb),
        in_specs=[
            pl.BlockSpec((SUBLANES, D_MODEL), lambda l, j: (0, 0)),
            pl.BlockSpec((None, D_MODEL, cb), lambda l, j: (l, 0, j)),
            pl.BlockSpec((None, 1, cb), lambda l, j: (l, 0, j)),
        ],
        out_specs=pl.BlockSpec((None, SUBLANES, cb), lambda l, j: (l, 0, j)),
        out_shape=jax.ShapeDtypeStruct((depth, SUBLANES, cols), F32),
        compiler_params=_cparams(("arbitrary", "arbitrary")),
        name="mods",
    )(cc, ada_w, ada_b.reshape(depth, 1, cols))


def _norm_mod(x, g, shift, scale):
    y = x * lax.rsqrt(jnp.mean(x * x, axis=-1, keepdims=True) + NORM_EPS) * g
    return y * (1.0 + scale) + shift


def _pre_kernel(xa_ref, xb_ref, mod_ref, g_ref, w_ref, pm_ref, u_ref, z_ref, gate_ref, wbf_ref, *,
                n_ctx_tiles):
    @pl.when(pl.program_id(0) == 0)
    def _():
        wbf_ref[...] = w_ref[...].astype(BF16)

    x = jnp.where(pl.program_id(0) < n_ctx_tiles, xa_ref[...], xb_ref[...])
    _pre_project(x, mod_ref, g_ref, wbf_ref, pm_ref, u_ref, z_ref, gate_ref)


def _pre_project(x, mod_ref, g_ref, wbf_ref, pm_ref, u_ref, z_ref, gate_ref):
    h = _norm_mod(x, g_ref[...], mod_ref[0:1, :], mod_ref[1:2, :])
    p = jnp.dot(h.astype(BF16), wbf_ref[...], preferred_element_type=F32)
    u_ref[...] = jnp.dot(pm_ref[...], p[:, :SSM_WIDTH].astype(BF16),
                         preferred_element_type=F32).astype(BF16)
    v1 = p[:, SSM_WIDTH:SSM_WIDTH + CONV_WIDTH]
    v2 = p[:, SSM_WIDTH + CONV_WIDTH:SSM_WIDTH + 2 * CONV_WIDTH]
    z_ref[...] = v1 * jax.nn.sigmoid(v2)
    gate_ref[...] = jax.nn.sigmoid(p[:, SSM_WIDTH + 2 * CONV_WIDTH:]).astype(BF16)


def _row_sources(rows_ctx, rows_lat, lat_offset, n_ctx_tiles):
    spec_ctx = pl.BlockSpec((TM, D_MODEL), lambda i, *_: (jnp.minimum(i, n_ctx_tiles - 1), 0))
    spec_lat = pl.BlockSpec((TM, D_MODEL),
                            lambda i, *_: (jnp.maximum(i - n_ctx_tiles, 0) + lat_offset, 0))
    return (rows_ctx, rows_lat), (spec_ctx, spec_lat)


def _pre(rows, n, mod, g, w_in, layer, pm, n_ctx_tiles):
    nt = n // TM
    kind = lambda i: (jnp.where(i < n_ctx_tiles, 0, 1), 0, 0)
    return pl.pallas_call(
        functools.partial(_pre_kernel, n_ctx_tiles=n_ctx_tiles),
        grid=(nt,),
        in_specs=[
            *rows[1],
            pl.BlockSpec((None, N_MOD, D_MODEL), kind),
            pl.BlockSpec((1, D_MODEL), lambda i: (0, 0)),
            pl.BlockSpec((None, D_MODEL, IN_COLS), lambda i: (layer, 0, 0), pipeline_mode=pl.Buffered(1)),
            pl.BlockSpec((TM, TM), lambda i: (0, 0)),
        ],
        out_specs=[
            pl.BlockSpec((TM, SSM_WIDTH), lambda i: (i, 0)),
            pl.BlockSpec((TM, CONV_WIDTH), lambda i: (i, 0)),
            pl.BlockSpec((TM, 2 * D_MODEL), lambda i: (i, 0)),
        ],
        out_shape=[
            jax.ShapeDtypeStruct((n, SSM_WIDTH), BF16),
            jax.ShapeDtypeStruct((n, CONV_WIDTH), F32),
            jax.ShapeDtypeStruct((n, 2 * D_MODEL), BF16),
        ],
        scratch_shapes=[pltpu.VMEM((D_MODEL, IN_COLS), BF16)],
        compiler_params=_cparams(("arbitrary",)),
        name="pre",
    )(*rows[0], mod, g, w_in, pm)


def _ssm_kernel(uf_ref, ub_ref, bw_ref, cw_ref, ar_ref, ai_ref, qr_ref, qi_ref,
                yf_ref, yb_ref, cr_ref, ci_ref, sf_ref, sb_ref, hf_ref, hb_ref):
    @pl.when(pl.program_id(0) == 0)
    def _():
        cr_ref[...] = jnp.zeros_like(cr_ref)
        ci_ref[...] = jnp.zeros_like(ci_ref)

    dirs = (0, 1)
    u_refs = (uf_ref, ub_ref)
    y_refs = (yf_ref, yb_ref)
    s_refs = (sf_ref, sb_ref)
    h_refs = (hf_ref, hb_ref)

    def group(d, i):
        return SEG - 1 - i if d == 1 else i

    def block(k):
        lanes = slice(LANES * k, LANES * (k + 1))
        for d in dirs:
            s_refs[d][...] = jnp.dot(u_refs[d][:, lanes], bw_ref[d, k], preferred_element_type=F32)

        ar = [ar_ref[d, k] for d in dirs]
        ai = [ai_ref[d, k] for d in dirs]

        def advance(carry, i):
            out = []
            for d in dirs:
                rows = pl.ds(pl.multiple_of(group(d, i) * SUBLANES, SUBLANES), SUBLANES)
                hr_d, hi_d = carry[2 * d], carry[2 * d + 1]
                out.append(ar[d] * hr_d - ai[d] * hi_d + s_refs[d][rows, 0:SB])
                out.append(ar[d] * hi_d + ai[d] * hr_d + s_refs[d][rows, SB:2 * SB])
            return tuple(out)

        zero = jnp.zeros((SUBLANES, SB), F32)
        ends = lax.fori_loop(0, SEG, lambda i, c: advance(c, i), (zero,) * 4, unroll=2)
        hr = [ends[0], ends[2]]
        hi = [ends[1], ends[3]]

        hpr = [None, None]
        hpi = [None, None]
        for d in dirs:
            asr = qr_ref[d, k, 0:1, :]
            asi = qi_ref[d, k, 0:1, :]
            pr_ = cr_ref[d, k]
            pi_ = ci_ref[d, k]
            prev_r = [None] * SUBLANES
            prev_i = [None] * SUBLANES
            order = range(SUBLANES - 1, -1, -1) if d == 1 else range(SUBLANES)
            for s in order:
                prev_r[s] = pr_
                prev_i[s] = pi_
                er = hr[d][s:s + 1, :]
                ei = hi[d][s:s + 1, :]
                pr_, pi_ = er + asr * pr_ - asi * pi_, ei + asr * pi_ + asi * pr_
            cr_ref[d, k] = pr_
            ci_ref[d, k] = pi_
            hpr[d] = jnp.concatenate(prev_r, axis=0)
            hpi[d] = jnp.concatenate(prev_i, axis=0)

        def rescan(j, carry):
            first = advance(carry, 2 * j)
            second = advance(first, 2 * j + 1)
            for d in dirs:
                lo, hi_ = (first, second) if d == 0 else (second, first)
                pair = j if d == 0 else SEG // 2 - 1 - j
                rows = pl.ds(pl.multiple_of(pair * 2 * SUBLANES, 2 * SUBLANES), 2 * SUBLANES)
                h_refs[d][rows, 0:SB] = jnp.concatenate([lo[2 * d], hi_[2 * d]], axis=0).astype(BF16)
                h_refs[d][rows, SB:2 * SB] = jnp.concatenate(
                    [lo[2 * d + 1], hi_[2 * d + 1]], axis=0).astype(BF16)
            return second

        lax.fori_loop(0, SEG // 2, rescan, (hpr[0], hpi[0], hpr[1], hpi[1]))

        for d in dirs:
            y_refs[d][:, lanes] = jnp.dot(h_refs[d][...], cw_ref[d, k],
                                          preferred_element_type=F32).astype(BF16)

    for k in range(KB):
        block(k)


def _ssm(u, bw, cw, ar, ai, qr, qi, n_ctx_tiles):
    n = u.shape[0]
    nt = n // TM

    def fwd(t):
        return (t, 0)

    def bwd(t):
        return (jnp.where(t < n_ctx_tiles, n_ctx_tiles - 1 - t, nt - 1 - (t - n_ctx_tiles)), 0)

    full = lambda a: pl.BlockSpec(a.shape, lambda t: (0,) * a.ndim)
    return pl.pallas_call(
        _ssm_kernel,
        grid=(nt,),
        in_specs=[
            pl.BlockSpec((TM, SSM_WIDTH), fwd),
            pl.BlockSpec((TM, SSM_WIDTH), bwd),
            full(bw), full(cw), full(ar), full(ai), full(qr), full(qi),
        ],
        out_specs=[pl.BlockSpec((TM, SSM_WIDTH), fwd), pl.BlockSpec((TM, SSM_WIDTH), bwd)],
        out_shape=[jax.ShapeDtypeStruct((n, SSM_WIDTH), BF16)] * 2,
        scratch_shapes=[
            pltpu.VMEM((2, KB, 1, SB), F32),
            pltpu.VMEM((2, KB, 1, SB), F32),
            pltpu.VMEM((TM, 2 * SB), F32),
            pltpu.VMEM((TM, 2 * SB), F32),
            pltpu.VMEM((TM, 2 * SB), BF16),
            pltpu.VMEM((TM, 2 * SB), BF16),
        ],
        compiler_params=_cparams(("arbitrary",)),
        name="ssm",
    )(u, u, bw, cw, ar, ai, qr, qi)


def _ssm_perm():
    p = jnp.arange(TM)
    src = SEG * (p % SUBLANES) + p // SUBLANES
    pm = (src[:, None] == jnp.arange(TM)[None, :]).astype(BF16)
    return pm, pm.T


def _ssm_params(a_re, a_im, log_dt, b_re, b_im, c_re, c_im):
    lam_re = jnp.minimum(a_re.astype(F32), -1e-4)
    lam_im = a_im.astype(F32)
    dt = jnp.exp(log_dt.astype(F32))[:, None]
    mag = jnp.exp(lam_re * dt)
    abar_re = mag * jnp.cos(lam_im * dt)
    abar_im = mag * jnp.sin(lam_im * dt)
    n_re, n_im = abar_re - 1.0, abar_im
    den = lam_re * lam_re + lam_im * lam_im
    coef_re = (n_re * lam_re + n_im * lam_im) / den
    coef_im = (n_im * lam_re - n_re * lam_im) / den
    br, bi = b_re.astype(F32), b_im.astype(F32)
    bbar_re = coef_re[..., None] * br - coef_im[..., None] * bi
    bbar_im = coef_re[..., None] * bi + coef_im[..., None] * br
    eye = jnp.eye(GPB, dtype=F32)

    def in_block(b):
        b = b.reshape(KB, GPB, SSM_STATE, SSM_GROUP)
        return jnp.einsum('kgpc,gh->kgchp', b, eye).reshape(KB, LANES, SB)

    def out_block(c):
        c = c.reshape(KB, GPB, SSM_GROUP, SSM_STATE)
        return jnp.einsum('kgcp,gh->kgphc', c, eye).reshape(KB, SB, LANES)

    bw = jnp.concatenate([in_block(bbar_re), in_block(bbar_im)], axis=2).astype(BF16)
    cw = jnp.concatenate([out_block(c_re.astype(F32)), -out_block(c_im.astype(F32))], axis=1).astype(BF16)
    qr, qi = abar_re, abar_im
    for _ in range(SEG - 1):
        qr, qi = qr * abar_re - qi * abar_im, qr * abar_im + qi * abar_re
    rep = lambda a: jnp.broadcast_to(a.reshape(KB, 1, SB), (KB, SUBLANES, SB))
    return bw, cw, rep(abar_re), rep(abar_im), rep(qr), rep(qi)


def _ln_silu(acc, g, b):
    mu = jnp.mean(acc, axis=-1, keepdims=True)
    xc = acc - mu
    y = xc * lax.rsqrt(jnp.mean(xc * xc, axis=-1, keepdims=True) + NORM_EPS) * g + b
    return y * jax.nn.sigmoid(y)


def _conv_rows_kernel(z_ref, w_ref, cb_ref, g_ref, b_ref, o_ref, zp_ref, *, n_ctx_tiles):
    i = pl.program_id(0)
    w = w_ref[...]
    cb = cb_ref[...]
    g = g_ref[...]
    b = b_ref[...]

    def run(seq):
        nseq = TM // seq
        stride = seq + 2 * PAD
        total = nseq * stride
        zeros = jnp.zeros((PAD, CONV_WIDTH), F32)
        for a in range(nseq):
            zp_ref[0, stride * a:stride * a + PAD, :] = zeros
            zp_ref[0, stride * a + PAD:stride * a + PAD + seq, :] = z_ref[seq * a:seq * (a + 1), :]
            zp_ref[0, stride * a + PAD + seq:stride * (a + 1), :] = zeros
        for k in range(1, SUBLANES):
            for r0 in range(0, total - SUBLANES, CHUNK):
                rows = min(CHUNK, total - SUBLANES - r0)
                zp_ref[k, r0:r0 + rows, :] = zp_ref[0, r0 + k:r0 + k + rows, :]
        for a in range(nseq):
            for r0 in range(0, seq, CHUNK):
                acc = jnp.zeros((CHUNK, CONV_WIDTH), F32) + cb
                for d in range(CONV_SIZE):
                    off = PAD + d - CONV_HALF
                    start = stride * a + r0 + off // SUBLANES * SUBLANES
                    acc = acc + w[d:d + 1, :] * zp_ref[off % SUBLANES, start:start + CHUNK, :]
                o_ref[seq * a + r0:seq * a + r0 + CHUNK, :] = _ln_silu(acc, g, b).astype(o_ref.dtype)

    @pl.when(i < n_ctx_tiles)
    def _():
        run(TM)

    @pl.when(i >= n_ctx_tiles)
    def _():
        run(GRID_W)


def _conv_rows(z, w, cb, g, b, n_ctx_tiles):
    n = z.shape[0]
    nt = n // TM
    vec = pl.BlockSpec((1, CONV_WIDTH), lambda i: (0, 0))
    return pl.pallas_call(
        functools.partial(_conv_rows_kernel, n_ctx_tiles=n_ctx_tiles),
        grid=(nt,),
        in_specs=[
            pl.BlockSpec((TM, CONV_WIDTH), lambda i: (i, 0)),
            pl.BlockSpec((CONV_SIZE, CONV_WIDTH), lambda i: (0, 0)),
            vec, vec, vec,
        ],
        out_specs=pl.BlockSpec((TM, CONV_WIDTH), lambda i: (i, 0)),
        out_shape=jax.ShapeDtypeStruct((n, CONV_WIDTH), BF16),
        scratch_shapes=[pltpu.VMEM((SUBLANES, TM // GRID_W * (GRID_W + 2 * PAD), CONV_WIDTH), F32)],
        compiler_params=_cparams(("arbitrary",)),
        name="conv_rows",
    )(z, w, cb, g, b)


WB = 16
RB = 16


def _conv_cols_kernel(z_ref, w_ref, cb_ref, g_ref, b_ref, o_ref, zp_ref, *, r_ctx, rows):
    w = w_ref[...]
    cb = cb_ref[...]
    g = g_ref[...]
    b = b_ref[...]
    zeros = jnp.zeros((CONV_HALF, WB, CONV_WIDTH), F32)
    zp_ref[0:CONV_HALF] = zeros
    zp_ref[CONV_HALF + rows:CONV_HALF + rows + CONV_HALF] = zeros
    zp_ref[CONV_HALF:CONV_HALF + rows] = z_ref[r_ctx:r_ctx + rows]
    o_ref[0:r_ctx] = jnp.zeros((r_ctx, WB, CONV_WIDTH), o_ref.dtype)

    def column(h, c):
        hs = slice(SUBLANES * h, SUBLANES * (h + 1))
        ls = slice(LANES * c, LANES * (c + 1))
        wv = [jnp.broadcast_to(w[d:d + 1, ls], (SUBLANES, LANES)) for d in range(CONV_SIZE)]
        bias = jnp.broadcast_to(cb[:, ls], (SUBLANES, LANES))

        def body(blk, carry):
            r0 = blk * RB
            acc = [bias] * RB
            for q in range(RB + CONV_SIZE - 1):
                zq = zp_ref[r0 + q, hs, ls]
                for j in range(RB):
                    d = q - j
                    if 0 <= d < CONV_SIZE:
                        acc[j] = acc[j] + wv[d] * zq
            for j in range(RB):
                zp_ref[r0 + j, hs, ls] = acc[j]
            return carry

        lax.fori_loop(0, rows // RB, body, 0)

    for h in range(WB // SUBLANES):
        for c in range(CONV_WIDTH // LANES):
            column(h, c)

    def norm(blk, carry):
        r0 = blk * SUBLANES
        v = zp_ref[pl.ds(r0, SUBLANES)].reshape(SUBLANES * WB, CONV_WIDTH)
        y = _ln_silu(v, g, b).astype(o_ref.dtype)
        o_ref[pl.ds(r_ctx + r0, SUBLANES)] = y.reshape(SUBLANES, WB, CONV_WIDTH)
        return carry

    lax.fori_loop(0, rows // SUBLANES, norm, 0)


def _conv_cols(z, w, cb, g, b, n_ctx):
    n = z.shape[0]
    r_all = n // GRID_W
    r_ctx = n_ctx // GRID_W
    rows = r_all - r_ctx
    z3 = z.reshape(r_all, GRID_W, CONV_WIDTH)
    vec = pl.BlockSpec((1, CONV_WIDTH), lambda j: (0, 0))
    out = pl.pallas_call(
        functools.partial(_conv_cols_kernel, r_ctx=r_ctx, rows=rows),
        grid=(GRID_W // WB,),
        in_specs=[
            pl.BlockSpec((r_all, WB, CONV_WIDTH), lambda j: (0, j, 0)),
            pl.BlockSpec((CONV_SIZE, CONV_WIDTH), lambda j: (0, 0)),
            vec, vec, vec,
        ],
        out_specs=pl.BlockSpec((r_all, WB, CONV_WIDTH), lambda j: (0, j, 0)),
        out_shape=jax.ShapeDtypeStruct((r_all, GRID_W, CONV_WIDTH), BF16),
        scratch_shapes=[pltpu.VMEM((rows + 2 * CONV_HALF, WB, CONV_WIDTH), F32)],
        compiler_params=_cparams(("arbitrary",)),
        name="conv_cols",
    )(z3, w, cb, g, b)
    return out.reshape(n, CONV_WIDTH)


def _route(lt):
    l = [lt[e:e + 1, :] for e in range(N_EXPERTS)]
    m = l[0]
    for e in range(1, N_EXPERTS):
        m = jnp.maximum(m, l[e])
    ex = [jnp.exp(v - m) for v in l]
    den = ex[0]
    for e in range(1, N_EXPERTS):
        den = den + ex[e]
    p = [v / den for v in ex]
    scores = []
    for gidx in range(N_EXPERT_GROUPS):
        q = p[gidx * EXPERTS_PER_GROUP:(gidx + 1) * EXPERTS_PER_GROUP]
        best = None
        for a in range(EXPERTS_PER_GROUP):
            for c in range(a + 1, EXPERTS_PER_GROUP):
                pair = q[a] + q[c]
                best = pair if best is None else jnp.maximum(best, pair)
        scores.append(best)
    bs = scores[0]
    bg = jnp.zeros_like(bs)
    for gidx in range(1, N_EXPERT_GROUPS):
        better = scores[gidx] > bs
        bs = jnp.where(better, scores[gidx], bs)
        bg = jnp.where(better, float(gidx), bg)
    q = []
    for a in range(EXPERTS_PER_GROUP):
        v = p[a]
        for gidx in range(1, N_EXPERT_GROUPS):
            v = jnp.where(bg == float(gidx), p[gidx * EXPERTS_PER_GROUP + a], v)
        q.append(v)
    p1 = q[0]
    i1 = jnp.zeros_like(p1)
    for a in range(1, EXPERTS_PER_GROUP):
        better = q[a] > p1
        p1 = jnp.where(better, q[a], p1)
        i1 = jnp.where(better, float(a), i1)
    p2 = jnp.full_like(p1, -1.0)
    i2 = jnp.zeros_like(p1)
    for a in range(EXPERTS_PER_GROUP):
        better = (q[a] > p2) & (i1 != float(a))
        p2 = jnp.where(better, q[a], p2)
        i2 = jnp.where(better, float(a), i2)
    tot = p1 + p2
    base = bg * float(EXPERTS_PER_GROUP)
    return base + i1, base + i2, p1 / tot, p2 / tot


def _merge_kernel(xa_ref, xb_ref, u_ref, yf_ref, yb_ref, bp_ref, gate_ref, mod_ref, d_ref, pmt_ref,
                  wav_ref, wag_ref, wb_ref, wo_ref, gf_ref, rwh_ref, rwl_ref, rb_ref,
                  xo_ref, h2_ref, r_ref, c_ref, cnt_ref, wa_bf, wb_bf, wo_bf, *, n_ctx_tiles):
    @pl.when(pl.program_id(0) == 0)
    def _():
        wa_bf[:, :D_MODEL] = wav_ref[...].astype(BF16)
        wa_bf[:, D_MODEL:] = wag_ref[...].astype(BF16)
        wb_bf[...] = wb_ref[...].astype(BF16)
        wo_bf[...] = wo_ref[...].astype(BF16)

    ya = jax.nn.gelu(yf_ref[...].astype(F32) + yb_ref[...].astype(F32)
                     + d_ref[...] * u_ref[...].astype(F32), approximate=True)
    ya = jnp.dot(pmt_ref[...], ya.astype(BF16), preferred_element_type=F32).astype(BF16)
    av = jnp.dot(ya, wa_bf[...], preferred_element_type=F32)
    a_out = av[:, :D_MODEL] * jax.nn.sigmoid(av[:, D_MODEL:])
    b_out = jnp.dot(bp_ref[...], wb_bf[...], preferred_element_type=F32)
    gate = gate_ref[...].astype(F32)
    merged = gate[:, :D_MODEL] * a_out + gate[:, D_MODEL:] * b_out
    mx = jnp.dot(merged.astype(BF16), wo_bf[...], preferred_element_type=F32)
    x = jnp.where(pl.program_id(0) < n_ctx_tiles, xa_ref[...], xb_ref[...])
    xn = x + mod_ref[2:3, :] * mx
    xo_ref[...] = xn
    h2 = _norm_mod(xn, gf_ref[...], mod_ref[3:4, :], mod_ref[4:5, :])
    h2_ref[...] = h2
    h_hi, h_lo = _split_bf16(h2)
    nt_dims = (((1,), (1,)), ((), ()))
    lt = lax.dot_general(rwh_ref[...], h_hi, nt_dims, preferred_element_type=F32)
    lt += lax.dot_general(rwh_ref[...], h_lo, nt_dims, preferred_element_type=F32)
    lt += lax.dot_general(rwl_ref[...], h_hi, nt_dims, preferred_element_type=F32)
    lt = lt + rb_ref[...]
    e1, e2, w1, w2 = _route(lt)
    @pl.when(pl.program_id(0) == 0)
    def _():
        cnt_ref[...] = jnp.zeros_like(cnt_ref)

    eid = lax.broadcasted_iota(jnp.int32, (N_EXPERTS, TM), 0).astype(F32)
    hit1 = eid == e1
    hit2 = eid == e2
    onehot = jnp.where(hit1 | hit2, 1.0, 0.0)
    before = lax.broadcasted_iota(jnp.int32, (TM, TM), 0) < lax.broadcasted_iota(jnp.int32, (TM, TM), 1)
    excl = jnp.dot(onehot.astype(BF16), jnp.where(before, 1.0, 0.0).astype(BF16),
                   preferred_element_type=F32)
    seen = excl + cnt_ref[:, 0:1]
    r1 = jnp.sum(jnp.where(hit1, seen, 0.0), axis=0, keepdims=True)
    r2 = jnp.sum(jnp.where(hit2, seen, 0.0), axis=0, keepdims=True)
    cnt_ref[...] = cnt_ref[...] + jnp.sum(onehot, axis=1, keepdims=True)
    c_ref[...] = cnt_ref[...]
    zero = jnp.zeros_like(e1)
    r_ref[...] = jnp.concatenate([e1, e2, w1, w2, r1, r2, zero, zero], axis=0)


def _merge(rows, n, u, yf, yb, bp, gates, mod, d_skip, pmt, w_a_val, w_a_gate, w_b_out, w_out, layer,
           gf, rwh, rwl, rb, n_ctx_tiles):
    nt = n // TM
    kind = lambda i: (jnp.where(i < n_ctx_tiles, 0, 1), 0, 0)
    tile = lambda w: pl.BlockSpec((TM, w), lambda i: (i, 0))
    full = lambda a: pl.BlockSpec(a.shape, lambda i: (0,) * a.ndim)
    lay = lambda a: pl.BlockSpec((None,) + a.shape[1:], lambda i: (layer, 0, 0))
    return pl.pallas_call(
        functools.partial(_merge_kernel, n_ctx_tiles=n_ctx_tiles),
        grid=(nt,),
        in_specs=[
            *rows[1], tile(SSM_WIDTH), tile(SSM_WIDTH), tile(SSM_WIDTH), tile(CONV_WIDTH),
            tile(2 * D_MODEL),
            pl.BlockSpec((None, N_MOD, D_MODEL), kind),
            full(d_skip), full(pmt), lay(w_a_val), lay(w_a_gate), lay(w_b_out), lay(w_out), full(gf),
            full(rwh), full(rwl), full(rb),
        ],
        out_specs=[
            tile(D_MODEL), tile(D_MODEL),
            pl.BlockSpec((SUBLANES, TM), lambda i: (0, i)),
            pl.BlockSpec((N_EXPERTS, LANES), lambda i: (0, 0)),
        ],
        out_shape=[
            jax.ShapeDtypeStruct((n, D_MODEL), F32),
            jax.ShapeDtypeStruct((n, D_MODEL), F32),
            jax.ShapeDtypeStruct((SUBLANES, n), F32),
            jax.ShapeDtypeStruct((N_EXPERTS, LANES), F32),
        ],
        scratch_shapes=[
            pltpu.VMEM((N_EXPERTS, LANES), F32),
            pltpu.VMEM((SSM_WIDTH, 2 * D_MODEL), BF16),
            pltpu.VMEM((CONV_WIDTH, D_MODEL), BF16),
            pltpu.VMEM((D_MODEL, D_MODEL), BF16),
        ],
        compiler_params=_cparams(("arbitrary",)),
        name="merge",
    )(*rows[0], u, yf, yb, bp, gates, mod, d_skip, pmt, w_a_val, w_a_gate, w_b_out, w_out, gf, rwh, rwl, rb)


TMM = 256


def _row_copy(src, i, dst, j, sem):
    return pltpu.make_async_copy(src.at[pl.ds(i, 1)], dst.at[pl.ds(j, 1)], sem)


def _invert_kernel(s1_ref, s2_ref, pad_lo_ref, pad_hi_ref, inv_ref, *, n):
    def clear(j, carry):
        inv_ref[j] = 0
        return carry

    for e in range(N_EXPERTS):
        lax.fori_loop(pad_lo_ref[e], pad_hi_ref[e], clear, 0)
    lax.fori_loop(pad_hi_ref[N_EXPERTS - 1], inv_ref.shape[0], clear, 0)

    def put(t, carry):
        inv_ref[s1_ref[t]] = t
        inv_ref[s2_ref[t]] = t
        return carry

    lax.fori_loop(0, n, put, 0, unroll=8)


def _invert(s1, s2, pad_lo, pad_hi, ns):
    return pl.pallas_call(
        functools.partial(_invert_kernel, n=s1.shape[0]),
        grid_spec=pltpu.PrefetchScalarGridSpec(
            num_scalar_prefetch=4,
            grid=(1,),
            in_specs=[],
            out_specs=pl.BlockSpec(memory_space=pltpu.SMEM),
        ),
        out_shape=jax.ShapeDtypeStruct((ns,), jnp.int32),
        compiler_params=_cparams(("arbitrary",)),
        name="moe_invert",
    )(s1, s2, pad_lo, pad_hi)


def _ffn_kernel(te_ref, nu_ref, inv_ref, h_hbm, wg_ref, wu_ref, wd_ref, y_ref, xbuf0, xbuf1, sem,
                wg_bf, wu_bf, wd_bf):
    i = pl.program_id(0)
    nu = nu_ref[0]
    xbuf = (xbuf0, xbuf1)

    def issue(tile, slot):
        for r in range(TMM):
            _row_copy(h_hbm, inv_ref[tile * TMM + r], xbuf[slot], r, sem.at[slot]).start()

    def drain(slot):
        for r in range(TMM):
            _row_copy(h_hbm, 0, xbuf[slot], 0, sem.at[slot]).wait()

    @pl.when((i == 0) & (nu > 0))
    def _():
        issue(0, 0)

    @pl.when((i < nu) & ((i == 0) | (te_ref[i] != te_ref[jnp.maximum(i - 1, 0)])))
    def _():
        wg_bf[...] = wg_ref[...].astype(BF16)
        wu_bf[...] = wu_ref[...].astype(BF16)
        wd_bf[...] = wd_ref[...].astype(BF16)

    def step(slot):
        issue(jnp.minimum(i + 1, nu - 1), 1 - slot)
        drain(slot)
        x = xbuf[slot][...].astype(BF16)
        hid = jnp.dot(x, wg_bf[...], preferred_element_type=F32)
        hid = hid * jax.nn.sigmoid(hid) * jnp.dot(x, wu_bf[...], preferred_element_type=F32)
        y_ref[...] = jnp.dot(hid.astype(BF16), wd_bf[...], preferred_element_type=F32)

        @pl.when(i == nu - 1)
        def _():
            drain(1 - slot)

    for slot in range(2):
        @pl.when((i % 2 == slot) & (i < nu))
        def _():
            step(slot)

    @pl.when(i >= nu)
    def _():
        y_ref[...] = jnp.zeros_like(y_ref)


def _ffn(h2, inv, tile_expert, n_used, wg, wu, wd, layer):
    ns = inv.shape[0]
    wspec = lambda *shape: pl.BlockSpec((None, None) + shape, lambda i, te, nu, iv: (layer, te[i], 0, 0))
    return pl.pallas_call(
        _ffn_kernel,
        grid_spec=pltpu.PrefetchScalarGridSpec(
            num_scalar_prefetch=3,
            grid=(ns // TMM,),
            in_specs=[
                pl.BlockSpec(memory_space=pl.ANY),
                wspec(D_MODEL, D_EXPERT), wspec(D_MODEL, D_EXPERT), wspec(D_EXPERT, D_MODEL),
            ],
            out_specs=pl.BlockSpec((TMM, D_MODEL), lambda i, te, nu, iv: (i, 0)),
            scratch_shapes=[pltpu.VMEM((TMM, D_MODEL), F32), pltpu.VMEM((TMM, D_MODEL), F32),
                            pltpu.SemaphoreType.DMA((2,)),
                            pltpu.VMEM((D_MODEL, D_EXPERT), BF16), pltpu.VMEM((D_MODEL, D_EXPERT), BF16),
                            pltpu.VMEM((D_EXPERT, D_MODEL), BF16)],
        ),
        out_shape=jax.ShapeDtypeStruct((ns, D_MODEL), F32),
        compiler_params=_cparams(("arbitrary",)),
        name="moe_ffn",
    )(tile_expert, n_used, inv, h2, wg, wu, wd)


def _combine_kernel(s1_ref, s2_ref, x_ref, rt_ref, mod_ref, g_ref, ys_hbm, o_ref, buf, sem, *,
                    first_tile, final):
    i = pl.program_id(0)
    nt = pl.num_programs(0)

    def issue(tile, slot):
        for r in range(TM):
            t = (first_tile + tile) * TM + r
            _row_copy(ys_hbm, s1_ref[t], buf.at[slot], r, sem.at[slot]).start()
            _row_copy(ys_hbm, s2_ref[t], buf.at[slot], TM + r, sem.at[slot]).start()

    def drain(slot):
        for r in range(2 * TM):
            _row_copy(ys_hbm, 0, buf.at[slot], 0, sem.at[slot]).wait()

    @pl.when(i == 0)
    def _():
        issue(0, 0)

    def step(slot):
        issue(jnp.minimum(i + 1, nt - 1), 1 - slot)
        drain(slot)
        rt = rt_ref[...]
        y = rt[:, 2:3] * buf[slot, 0:TM, :] + rt[:, 3:4] * buf[slot, TM:2 * TM, :]
        out = x_ref[...] + mod_ref[5:6, :] * y
        if final:
            out = out * lax.rsqrt(jnp.mean(out * out, axis=-1, keepdims=True) + NORM_EPS) * g_ref[...]
        o_ref[...] = out

        @pl.when(i == nt - 1)
        def _():
            drain(1 - slot)

    for slot in range(2):
        @pl.when(i % 2 == slot)
        def _():
            step(slot)


def _combine(xn, rt, mod, g, ys, s1, s2, n_ctx_tiles, final):
    n = xn.shape[0]
    first = n_ctx_tiles if final else 0
    tiles = n // TM - first
    kind = lambda i, a, b: (jnp.where(i + first < n_ctx_tiles, 0, 1), 0, 0)
    return pl.pallas_call(
        functools.partial(_combine_kernel, first_tile=first, final=final),
        grid_spec=pltpu.PrefetchScalarGridSpec(
            num_scalar_prefetch=2,
            grid=(tiles,),
            in_specs=[
                pl.BlockSpec((TM, D_MODEL), lambda i, a, b: (i + first, 0)),
                pl.BlockSpec((TM, SUBLANES), lambda i, a, b: (i + first, 0)),
                pl.BlockSpec((None, N_MOD, D_MODEL), kind),
                pl.BlockSpec((1, D_MODEL), lambda i, a, b: (0, 0)),
                pl.BlockSpec(memory_space=pl.ANY),
            ],
            out_specs=pl.BlockSpec((TM, D_MODEL), lambda i, a, b: (i, 0)),
            scratch_shapes=[
                pltpu.VMEM((2, 2 * TM, D_MODEL), F32),
                pltpu.SemaphoreType.DMA((2,)),
            ],
        ),
        out_shape=jax.ShapeDtypeStruct((tiles * TM, D_MODEL), F32),
        compiler_params=_cparams(("arbitrary",)),
        name="moe_combine",
    )(s1, s2, xn, rt, mod, g, ys)


def _pre_gather_kernel(s1_ref, s2_ref, x_ref, rt_ref, modp_ref, ys_hbm, mod_ref, g_ref, w_ref, pm_ref,
                       xc_ref, u_ref, z_ref, gate_ref, buf, sem, wbf_ref):
    i = pl.program_id(0)
    nt = pl.num_programs(0)

    @pl.when(i == 0)
    def _():
        wbf_ref[...] = w_ref[...].astype(BF16)

    def issue(tile, slot):
        for r in range(TM):
            t = tile * TM + r
            _row_copy(ys_hbm, s1_ref[t], buf.at[slot], r, sem.at[slot]).start()
            _row_copy(ys_hbm, s2_ref[t], buf.at[slot], TM + r, sem.at[slot]).start()

    def drain(slot):
        for r in range(2 * TM):
            _row_copy(ys_hbm, 0, buf.at[slot], 0, sem.at[slot]).wait()

    @pl.when(i == 0)
    def _():
        issue(0, 0)

    def step(slot):
        issue(jnp.minimum(i + 1, nt - 1), 1 - slot)
        drain(slot)
        rt = rt_ref[...]
        y = rt[:, 2:3] * buf[slot, 0:TM, :] + rt[:, 3:4] * buf[slot, TM:2 * TM, :]
        x = x_ref[...] + modp_ref[5:6, :] * y
        xc_ref[...] = x
        _pre_project(x, mod_ref, g_ref, wbf_ref, pm_ref, u_ref, z_ref, gate_ref)

        @pl.when(i == nt - 1)
        def _():
            drain(1 - slot)

    for slot in range(2):
        @pl.when(i % 2 == slot)
        def _():
            step(slot)


def _pre_gather(pending, mod, g, w_in, layer, pm, n_ctx_tiles):
    xn, rt, mod_prev, ys, s1, s2 = pending
    n = xn.shape[0]
    kind = lambda i, a, b: (jnp.where(i < n_ctx_tiles, 0, 1), 0, 0)
    tile = lambda w: pl.BlockSpec((TM, w), lambda i, a, b: (i, 0))
    return pl.pallas_call(
        _pre_gather_kernel,
        grid_spec=pltpu.PrefetchScalarGridSpec(
            num_scalar_prefetch=2,
            grid=(n // TM,),
            in_specs=[
                tile(D_MODEL), tile(SUBLANES),
                pl.BlockSpec((None, N_MOD, D_MODEL), kind),
                pl.BlockSpec(memory_space=pl.ANY),
                pl.BlockSpec((None, N_MOD, D_MODEL), kind),
                pl.BlockSpec((1, D_MODEL), lambda i, a, b: (0, 0)),
                pl.BlockSpec((None, D_MODEL, IN_COLS), lambda i, a, b: (layer, 0, 0),
                             pipeline_mode=pl.Buffered(1)),
                pl.BlockSpec((TM, TM), lambda i, a, b: (0, 0)),
            ],
            out_specs=[tile(D_MODEL), tile(SSM_WIDTH), tile(CONV_WIDTH), tile(2 * D_MODEL)],
            scratch_shapes=[
                pltpu.VMEM((2, 2 * TM, D_MODEL), F32),
                pltpu.SemaphoreType.DMA((2,)),
                pltpu.VMEM((D_MODEL, IN_COLS), BF16),
            ],
        ),
        out_shape=[
            jax.ShapeDtypeStruct((n, D_MODEL), F32),
            jax.ShapeDtypeStruct((n, SSM_WIDTH), BF16),
            jax.ShapeDtypeStruct((n, CONV_WIDTH), F32),
            jax.ShapeDtypeStruct((n, 2 * D_MODEL), BF16),
        ],
        compiler_params=_cparams(("arbitrary",)),
        name="pre_gather",
    )(s1, s2, xn, rt, mod_prev, ys, mod, g, w_in, pm)


def _moe(xn, h2, r, counts, mod, wg, wu, wd, layer, final_g, n_ctx_tiles, final):
    n = xn.shape[0]
    ns = 2 * n + N_EXPERTS * TMM
    cnt = counts[:, 0].astype(jnp.int32)
    padded = (cnt + TMM - 1) // TMM * TMM
    ends = jnp.cumsum(padded)
    off = ends - padded
    experts = jnp.arange(N_EXPERTS, dtype=jnp.int32)

    def slot(e_row, rank_row):
        hit = e_row.astype(jnp.int32)[:, None] == experts[None, :]
        return jnp.sum(jnp.where(hit, off[None, :], 0), axis=1) + rank_row.astype(jnp.int32)

    s1 = slot(r[0], r[4])
    s2 = slot(r[1], r[5])
    tile_start = jnp.arange(ns // TMM, dtype=jnp.int32) * TMM
    tile_expert = jnp.minimum(jnp.sum(ends[None, :] <= tile_start[:, None], axis=1),
                              N_EXPERTS - 1).astype(jnp.int32)
    n_used = (ends[-1:] // TMM).astype(jnp.int32)
    inv = _invert(s1, s2, off + cnt, ends, ns)
    ys = _ffn(h2, inv, tile_expert, n_used, wg, wu, wd, layer)
    if final:
        return _combine(xn, r.T, mod, final_g, ys, s1, s2, n_ctx_tiles, final)
    return xn, r.T, mod, ys, s1, s2


def kernel(x, c, ctx, c_ctx, ada_w, ada_b, norm_mix_g, norm_ffn_g, w_in, ssm_a_re, ssm_a_im, ssm_log_dt, ssm_b_re, ssm_b_im, ssm_c_re, ssm_c_im, ssm_d, w_a_val, w_a_gate, conv_w, conv_b, conv_ln_g, conv_ln_b, w_b_out, w_out, router_w, router_b, w_exp_gate, w_exp_up, w_exp_down, final_norm_g):
    depth = ada_w.shape[0]
    bsz, seq, _ = x.shape
    n_ctx = ctx.shape[1]
    assert bsz == 1 and n_ctx == TM and seq % TM == 0 and seq % GRID_W == 0
    n_ctx_tiles = n_ctx // TM
    n = n_ctx + seq

    rows = _row_sources(ctx[0], x[0], 0, n_ctx_tiles)
    cc = jnp.zeros((SUBLANES, D_MODEL), F32).at[0].set(c_ctx).at[1].set(c[0])
    mods = _mods(cc, ada_w, ada_b)[:, :2].reshape(depth, 2, N_MOD, D_MODEL)

    rwt = router_w.T
    rwh, rwl = _split_bf16(rwt)
    rb = router_b.reshape(N_EXPERTS, 1)
    pm, pmt = _ssm_perm()

    pending = None
    for l in range(depth):
        last = l == depth - 1
        g_mix = norm_mix_g[l].reshape(1, -1)
        if pending is None:
            u, z, gates = _pre(rows, n, mods[l], g_mix, w_in, l, pm, n_ctx_tiles)
        else:
            xc, u, z, gates = _pre_gather(pending, mods[l], g_mix, w_in, l, pm, n_ctx_tiles)
            rows = _row_sources(xc, xc, n_ctx_tiles, n_ctx_tiles)
        prm =[_ssm_params(ssm_a_re[l, k], ssm_a_im[l, k], ssm_log_dt[l, k], ssm_b_re[l, k],
                           ssm_b_im[l, k], ssm_c_re[l, k], ssm_c_im[l, k]) for k in range(2)]
        ys = _ssm(u, *[jnp.stack(pair) for pair in zip(*prm)], n_ctx_tiles)
        vec = lambda a: a.reshape(1, -1)
        conv_args = (conv_w[l], vec(conv_b[l]), vec(conv_ln_g[l]), vec(conv_ln_b[l]))
        if l % 2 == 0:
            bp = _conv_rows(z, *conv_args, n_ctx_tiles)
        else:
            assert last, "column conv over the context prefix is not needed by the last layer only"
            bp = _conv_cols(z, *conv_args, n_ctx)
        xn, h2, r, counts = _merge(rows, n, u, ys[0], ys[1], bp, gates, mods[l], vec(ssm_d[l]), pmt,
                                   w_a_val, w_a_gate, w_b_out, w_out, l, vec(norm_ffn_g[l]),
                                   rwh, rwl, rb, n_ctx_tiles)
        pending = _moe(xn, h2, r, counts, mods[l], w_exp_gate, w_exp_up, w_exp_down, l,
                       vec(final_norm_g), n_ctx_tiles, final=last)
    return pending[None]
```

```python
import functools
import math

import jax
import jax.numpy as jnp
from jax import lax
from jax.experimental import pallas as pl
from jax.experimental.pallas import tpu as pltpu

F32 = jnp.float32
BF16 = jnp.bfloat16

D_MODEL = 1024
GRID_W = 64
SSM_WIDTH = 512
SSM_GROUP = 16
SSM_GROUPS = SSM_WIDTH // SSM_GROUP
SSM_STATE = 64
CONV_WIDTH = 512
CONV_SIZE = 31
CONV_HALF = CONV_SIZE // 2
IN_COLS = SSM_WIDTH + 2 * CONV_WIDTH + 2 * D_MODEL
N_EXPERTS = 16
N_EXPERT_GROUPS = 4
EXPERTS_PER_GROUP = N_EXPERTS // N_EXPERT_GROUPS
D_EXPERT = 512
N_MOD = 6
NORM_EPS = 1e-6

TM = 256
LANES = 128
SUBLANES = 8
SEG = TM // SUBLANES
KB = SSM_WIDTH // LANES
GPB = LANES // SSM_GROUP
SB = GPB * SSM_STATE
PAD = 16
CHUNK = 32
VMEM_LIMIT = 56 * 1024 * 1024


def _cparams(sem):
    return pltpu.CompilerParams(dimension_semantics=sem, vmem_limit_bytes=VMEM_LIMIT)


def _split_bf16(a):
    hi = a.astype(BF16)
    lo = (a - hi.astype(F32)).astype(BF16)
    return hi, lo


def _mods_kernel(c_ref, w_ref, b_ref, o_ref):
    s = c_ref[...]
    s = s * jax.nn.sigmoid(s)
    s_hi, s_lo = _split_bf16(s)
    w_hi, w_lo = _split_bf16(w_ref[...])
    acc = jnp.dot(s_hi, w_hi, preferred_element_type=F32)
    acc += jnp.dot(s_lo, w_hi, preferred_element_type=F32)
    acc += jnp.dot(s_hi, w_lo, preferred_element_type=F32)
    o_ref[...] = acc + b_ref[...]


def _mods(cc, ada_w, ada_b):
    depth = ada_w.shape[0]
    cols = ada_w.shape[2]
    cb = 1536
    return pl.pallas_call(
        _mods_kernel,
        grid=(depth, cols // cb),
        in_specs=[
            pl.BlockSpec((SUBLANES, D_MODEL), lambda l, j: (0, 0)),
            pl.BlockSpec((None, D_MODEL, cb), lambda l, j: (l, 0, j)),
            pl.BlockSpec((None, 1, cb), lambda l, j: (l, 0, j)),
        ],
        out_specs=pl.BlockSpec((None, SUBLANES, cb), lambda l, j: (l, 0, j)),
        out_shape=jax.ShapeDtypeStruct((depth, SUBLANES, cols), F32),
        compiler_params=_cparams(("arbitrary", "arbitrary")),
        name="mods",
    )(cc, ada_w, ada_b.reshape(depth, 1, cols))


def _norm_mod(x, g, shift, scale):
    y = x * lax.rsqrt(jnp.mean(x * x, axis=-1, keepdims=True) + NORM_EPS) * g
    return y * (1.0 + scale) + shift


def _pre_kernel(xa_ref, xb_ref, mod_ref, g_ref, w_ref, pm_ref, u_ref, z_ref, gate_ref, wbf_ref, *,
                n_ctx_tiles):
    @pl.when(pl.program_id(0) == 0)
    def _():
        wbf_ref[...] = w_ref[...].astype(BF16)

    x = jnp.where(pl.program_id(0) < n_ctx_tiles, xa_ref[...], xb_ref[...])
    _pre_project(x, mod_ref, g_ref, wbf_ref, pm_ref, u_ref, z_ref, gate_ref)


def _pre_project(x, mod_ref, g_ref, wbf_ref, pm_ref, u_ref, z_ref, gate_ref):
    h = _norm_mod(x, g_ref[...], mod_ref[0:1, :], mod_ref[1:2, :])
    p = jnp.dot(h.astype(BF16), wbf_ref[...], preferred_element_type=F32)
    u_ref[...] = jnp.dot(pm_ref[...], p[:, :SSM_WIDTH].astype(BF16),
                         preferred_element_type=F32).astype(BF16)
    v1 = p[:, SSM_WIDTH:SSM_WIDTH + CONV_WIDTH]
    v2 = p[:, SSM_WIDTH + CONV_WIDTH:SSM_WIDTH + 2 * CONV_WIDTH]
    z_ref[...] = v1 * jax.nn.sigmoid(v2)
    gate_ref[...] = jax.nn.sigmoid(p[:, SSM_WIDTH + 2 * CONV_WIDTH:]).astype(BF16)


def _row_sources(rows_ctx, rows_lat, lat_offset, n_ctx_tiles):
    spec_ctx = pl.BlockSpec((TM, D_MODEL), lambda i, *_: (jnp.minimum(i, n_ctx_tiles - 1), 0))
    spec_lat = pl.BlockSpec((TM, D_MODEL),
                            lambda i, *_: (jnp.maximum(i - n_ctx_tiles, 0) + lat_offset, 0))
    return (rows_ctx, rows_lat), (spec_ctx, spec_lat)


def _pre(rows, n, mod, g, w_in, layer, pm, n_ctx_tiles):
    nt = n // TM
    kind = lambda i: (jnp.where(i < n_ctx_tiles, 0, 1), 0, 0)
    return pl.pallas_call(
        functools.partial(_pre_kernel, n_ctx_tiles=n_ctx_tiles),
        grid=(nt,),
        in_specs=[
            *rows[1],
            pl.BlockSpec((None, N_MOD, D_MODEL), kind),
            pl.BlockSpec((1, D_MODEL), lambda i: (0, 0)),
            pl.BlockSpec((None, D_MODEL, IN_COLS), lambda i: (layer, 0, 0), pipeline_mode=pl.Buffered(1)),
            pl.BlockSpec((TM, TM), lambda i: (0, 0)),
        ],
        out_specs=[
            pl.BlockSpec((TM, SSM_WIDTH), lambda i: (i, 0)),
            pl.BlockSpec((TM, CONV_WIDTH), lambda i: (i, 0)),
            pl.BlockSpec((TM, 2 * D_MODEL), lambda i: (i, 0)),
        ],
        out_shape=[
            jax.ShapeDtypeStruct((n, SSM_WIDTH), BF16),
            jax.ShapeDtypeStruct((n, CONV_WIDTH), F32),
            jax.ShapeDtypeStruct((n, 2 * D_MODEL), BF16),
        ],
        scratch_shapes=[pltpu.VMEM((D_MODEL, IN_COLS), BF16)],
        compiler_params=_cparams(("arbitrary",)),
        name="pre",
    )(*rows[0], mod, g, w_in, pm)


def _ssm_kernel(uf_ref, ub_ref, bw_ref, cw_ref, ar_ref, ai_ref, qr_ref, qi_ref,
                yf_ref, yb_ref, cr_ref, ci_ref, sf_ref, sb_ref, hf_ref, hb_ref):
    @pl.when(pl.program_id(0) == 0)
    def _():
        cr_ref[...] = jnp.zeros_like(cr_ref)
        ci_ref[...] = jnp.zeros_like(ci_ref)

    dirs = (0, 1)
    u_refs = (uf_ref, ub_ref)
    y_refs = (yf_ref, yb_ref)
    s_refs = (sf_ref, sb_ref)
    h_refs = (hf_ref, hb_ref)

    def group(d, i):
        return SEG - 1 - i if d == 1 else i

    def block(k):
        lanes = slice(LANES * k, LANES * (k + 1))
        for d in dirs:
            s_refs[d][...] = jnp.dot(u_refs[d][:, lanes], bw_ref[d, k], preferred_element_type=F32)

        ar = [ar_ref[d, k] for d in dirs]
        ai = [ai_ref[d, k] for d in dirs]

        def advance(carry, i):
            out = []
            for d in dirs:
                rows = pl.ds(pl.multiple_of(group(d, i) * SUBLANES, SUBLANES), SUBLANES)
                hr_d, hi_d = carry[2 * d], carry[2 * d + 1]
                out.append(ar[d] * hr_d - ai[d] * hi_d + s_refs[d][rows, 0:SB])
                out.append(ar[d] * hi_d + ai[d] * hr_d + s_refs[d][rows, SB:2 * SB])
            return tuple(out)

        zero = jnp.zeros((SUBLANES, SB), F32)
        ends = lax.fori_loop(0, SEG, lambda i, c: advance(c, i), (zero,) * 4, unroll=2)
        hr = [ends[0], ends[2]]
        hi = [ends[1], ends[3]]

        hpr = [None, None]
        hpi = [None, None]
        for d in dirs:
            asr = qr_ref[d, k, 0:1, :]
            asi = qi_ref[d, k, 0:1, :]
            pr_ = cr_ref[d, k]
            pi_ = ci_ref[d, k]
            prev_r = [None] * SUBLANES
            prev_i = [None] * SUBLANES
            order = range(SUBLANES - 1, -1, -1) if d == 1 else range(SUBLANES)
            for s in order:
                prev_r[s] = pr_
                prev_i[s] = pi_
                er = hr[d][s:s + 1, :]
                ei = hi[d][s:s + 1, :]
                pr_, pi_ = er + asr * pr_ - asi * pi_, ei + asr * pi_ + asi * pr_
            cr_ref[d, k] = pr_
            ci_ref[d, k] = pi_
            hpr[d] = jnp.concatenate(prev_r, axis=0)
            hpi[d] = jnp.concatenate(prev_i, axis=0)

        def rescan(j, carry):
            first = advance(carry, 2 * j)
            second = advance(first, 2 * j + 1)
            for d in dirs:
                lo, hi_ = (first, second) if d == 0 else (second, first)
                pair = j if d == 0 else SEG // 2 - 1 - j
                rows = pl.ds(pl.multiple_of(pair * 2 * SUBLANES, 2 * SUBLANES), 2 * SUBLANES)
                h_refs[d][rows, 0:SB] = jnp.concatenate([lo[2 * d], hi_[2 * d]], axis=0).astype(BF16)
                h_refs[d][rows, SB:2 * SB] = jnp.concatenate(
                    [lo[2 * d + 1], hi_[2 * d + 1]], axis=0).astype(BF16)
            return second

        lax.fori_loop(0, SEG // 2, rescan, (hpr[0], hpi[0], hpr[1], hpi[1]))

        for d in dirs:
            y_refs[d][:, lanes] = jnp.dot(h_refs[d][...], cw_ref[d, k],
                                          preferred_element_type=F32).astype(BF16)

    for k in range(KB):
        block(k)


def _ssm(u, bw, cw, ar, ai, qr, qi, n_ctx_tiles):
    n = u.shape[0]
    nt = n // TM

    def fwd(t):
        return (t, 0)

    def bwd(t):
        return (jnp.where(t < n_ctx_tiles, n_ctx_tiles - 1 - t, nt - 1 - (t - n_ctx_tiles)), 0)

    full = lambda a: pl.BlockSpec(a.shape, lambda t: (0,) * a.ndim)
    return pl.pallas_call(
        _ssm_kernel,
        grid=(nt,),
        in_specs=[
            pl.BlockSpec((TM, SSM_WIDTH), fwd),
            pl.BlockSpec((TM, SSM_WIDTH), bwd),
            full(bw), full(cw), full(ar), full(ai), full(qr), full(qi),
        ],
        out_specs=[pl.BlockSpec((TM, SSM_WIDTH), fwd), pl.BlockSpec((TM, SSM_WIDTH), bwd)],
        out_shape=[jax.ShapeDtypeStruct((n, SSM_WIDTH), BF16)] * 2,
        scratch_shapes=[
            pltpu.VMEM((2, KB, 1, SB), F32),
            pltpu.VMEM((2, KB, 1, SB), F32),
            pltpu.VMEM((TM, 2 * SB), F32),
            pltpu.VMEM((TM, 2 * SB), F32),
            pltpu.VMEM((TM, 2 * SB), BF16),
            pltpu.VMEM((TM, 2 * SB), BF16),
        ],
        compiler_params=_cparams(("arbitrary",)),
        name="ssm",
    )(u, u, bw, cw, ar, ai, qr, qi)


def _ssm_perm():
    p = jnp.arange(TM)
    src = SEG * (p % SUBLANES) + p // SUBLANES
    pm = (src[:, None] == jnp.arange(TM)[None, :]).astype(BF16)
    return pm, pm.T


def _ssm_params(a_re, a_im, log_dt, b_re, b_im, c_re, c_im):
    lam_re = jnp.minimum(a_re.astype(F32), -1e-4)
    lam_im = a_im.astype(F32)
    dt = jnp.exp(log_dt.astype(F32))[:, None]
    mag = jnp.exp(lam_re * dt)
    abar_re = mag * jnp.cos(lam_im * dt)
    abar_im = mag * jnp.sin(lam_im * dt)
    n_re, n_im = abar_re - 1.0, abar_im
    den = lam_re * lam_re + lam_im * lam_im
    coef_re = (n_re * lam_re + n_im * lam_im) / den
    coef_im = (n_im * lam_re - n_re * lam_im) / den
    br, bi = b_re.astype(F32), b_im.astype(F32)
    bbar_re = coef_re[..., None] * br - coef_im[..., None] * bi
    bbar_im = coef_re[..., None] * bi + coef_im[..., None] * br
    eye = jnp.eye(GPB, dtype=F32)

    def in_block(b):
        b = b.reshape(KB, GPB, SSM_STATE, SSM_GROUP)
        return jnp.einsum('kgpc,gh->kgchp', b, eye).reshape(KB, LANES, SB)

    def out_block(c):
        c = c.reshape(KB, GPB, SSM_GROUP, SSM_STATE)
        return jnp.einsum('kgcp,gh->kgphc', c, eye).reshape(KB, SB, LANES)

    bw = jnp.concatenate([in_block(bbar_re), in_block(bbar_im)], axis=2).astype(BF16)
    cw = jnp.concatenate([out_block(c_re.astype(F32)), -out_block(c_im.astype(F32))], axis=1).astype(BF16)
    qr, qi = abar_re, abar_im
    for _ in range(SEG - 1):
        qr, qi = qr * abar_re - qi * abar_im, qr * abar_im + qi * abar_re
    rep = lambda a: jnp.broadcast_to(a.reshape(KB, 1, SB), (KB, SUBLANES, SB))
    return bw, cw, rep(abar_re), rep(abar_im), rep(qr), rep(qi)


def _ln_silu(acc, g, b):
    mu = jnp.mean(acc, axis=-1, keepdims=True)
    xc = acc - mu
    y = xc * lax.rsqrt(jnp.mean(xc * xc, axis=-1, keepdims=True) + NORM_EPS) * g + b
    return y * jax.nn.sigmoid(y)


def _conv_rows_kernel(z_ref, w_ref, cb_ref, g_ref, b_ref, o_ref, zp_ref, *, n_ctx_tiles):
    i = pl.program_id(0)
    w = w_ref[...]
    cb = cb_ref[...]
    g = g_ref[...]
    b = b_ref[...]

    def run(seq):
        nseq = TM // seq
        stride = seq + 2 * PAD
        total = nseq * stride
        zeros = jnp.zeros((PAD, CONV_WIDTH), F32)
        for a in range(nseq):
            zp_ref[0, stride * a:stride * a + PAD, :] = zeros
            zp_ref[0, stride * a + PAD:stride * a + PAD + seq, :] = z_ref[seq * a:seq * (a + 1), :]
            zp_ref[0, stride * a + PAD + seq:stride * (a + 1), :] = zeros
        for k in range(1, SUBLANES):
            for r0 in range(0, total - SUBLANES, CHUNK):
                rows = min(CHUNK, total - SUBLANES - r0)
                zp_ref[k, r0:r0 + rows, :] = zp_ref[0, r0 + k:r0 + k + rows, :]
        for a in range(nseq):
            for r0 in range(0, seq, CHUNK):
                acc = jnp.zeros((CHUNK, CONV_WIDTH), F32) + cb
                for d in range(CONV_SIZE):
                    off = PAD + d - CONV_HALF
                    start = stride * a + r0 + off // SUBLANES * SUBLANES
                    acc = acc + w[d:d + 1, :] * zp_ref[off % SUBLANES, start:start + CHUNK, :]
                o_ref[seq * a + r0:seq * a + r0 + CHUNK, :] = _ln_silu(acc, g, b).astype(o_ref.dtype)

    @pl.when(i < n_ctx_tiles)
    def _():
        run(TM)

    @pl.when(i >= n_ctx_tiles)
    def _():
        run(GRID_W)


def _conv_rows(z, w, cb, g, b, n_ctx_tiles):
    n = z.shape[0]
    nt = n // TM
    vec = pl.BlockSpec((1, CONV_WIDTH), lambda i: (0, 0))
    return pl.pallas_call(
        functools.partial(_conv_rows_kernel, n_ctx_tiles=n_ctx_tiles),
        grid=(nt,),
        in_specs=[
            pl.BlockSpec((TM, CONV_WIDTH), lambda i: (i, 0)),
            pl.BlockSpec((CONV_SIZE, CONV_WIDTH), lambda i: (0, 0)),
            vec, vec, vec,
        ],
        out_specs=pl.BlockSpec((TM, CONV_WIDTH), lambda i: (i, 0)),
        out_shape=jax.ShapeDtypeStruct((n, CONV_WIDTH), BF16),
        scratch_shapes=[pltpu.VMEM((SUBLANES, TM // GRID_W * (GRID_W + 2 * PAD), CONV_WIDTH), F32)],
        compiler_params=_cparams(("arbitrary",)),
        name="conv_rows",
    )(z, w, cb, g, b)


WB = 16
RB = 16


def _conv_cols_kernel(z_ref, w_ref, cb_ref, g_ref, b_ref, o_ref, zp_ref, *, r_ctx, rows):
    w = w_ref[...]
    cb = cb_ref[...]
    g = g_ref[...]
    b = b_ref[...]
    zeros = jnp.zeros((CONV_HALF, WB, CONV_WIDTH), F32)
    zp_ref[0:CONV_HALF] = zeros
    zp_ref[CONV_HALF + rows:CONV_HALF + rows + CONV_HALF] = zeros
    zp_ref[CONV_HALF:CONV_HALF + rows] = z_ref[r_ctx:r_ctx + rows]
    o_ref[0:r_ctx] = jnp.zeros((r_ctx, WB, CONV_WIDTH), o_ref.dtype)

    def column(h, c):
        hs = slice(SUBLANES * h, SUBLANES * (h + 1))
        ls = slice(LANES * c, LANES * (c + 1))
        wv = [jnp.broadcast_to(w[d:d + 1, ls], (SUBLANES, LANES)) for d in range(CONV_SIZE)]
        bias = jnp.broadcast_to(cb[:, ls], (SUBLANES, LANES))

        def body(blk, carry):
            r0 = blk * RB
            acc = [bias] * RB
            for q in range(RB + CONV_SIZE - 1):
                zq = zp_ref[r0 + q, hs, ls]
                for j in range(RB):
                    d = q - j
                    if 0 <= d < CONV_SIZE:
                        acc[j] = acc[j] + wv[d] * zq
            for j in range(RB):
                zp_ref[r0 + j, hs, ls] = acc[j]
            return carry

        lax.fori_loop(0, rows // RB, body, 0)

    for h in range(WB // SUBLANES):
        for c in range(CONV_WIDTH // LANES):
            column(h, c)

    def norm(blk, carry):
        r0 = blk * SUBLANES
        v = zp_ref[pl.ds(r0, SUBLANES)].reshape(SUBLANES * WB, CONV_WIDTH)
        y = _ln_silu(v, g, b).astype(o_ref.dtype)
        o_ref[pl.ds(r_ctx + r0, SUBLANES)] = y.reshape(SUBLANES, WB, CONV_WIDTH)
        return carry

    lax.fori_loop(0, rows // SUBLANES, norm, 0)


def _conv_cols(z, w, cb, g, b, n_ctx):
    n = z.shape[0]
    r_all = n // GRID_W
    r_ctx = n_ctx // GRID_W
    rows = r_all - r_ctx
    z3 = z.reshape(r_all, GRID_W, CONV_WIDTH)
    vec = pl.BlockSpec((1, CONV_WIDTH), lambda j: (0, 0))
    out = pl.pallas_call(
        functools.partial(_conv_cols_kernel, r_ctx=r_ctx, rows=rows),
        grid=(GRID_W // WB,),
        in_specs=[
            pl.BlockSpec((r_all, WB, CONV_WIDTH), lambda j: (0, j, 0)),
            pl.BlockSpec((CONV_SIZE, CONV_WIDTH), lambda j: (0, 0)),
            vec, vec, vec,
        ],
        out_specs=pl.BlockSpec((r_all, WB, CONV_WIDTH), lambda j: (0, j, 0)),
        out_shape=jax.ShapeDtypeStruct((r_all, GRID_W, CONV_WIDTH), BF16),
        scratch_shapes=[pltpu.VMEM((rows + 2 * CONV_HALF, WB, CONV_WIDTH), F32)],
        compiler_params=_cparams(("arbitrary",)),
        name="conv_cols",
    )(z3, w, cb, g, b)
    return out.reshape(n, CONV_WIDTH)


def _route(lt):
    l = [lt[e:e + 1, :] for e in range(N_EXPERTS)]
    m = l[0]
    for e in range(1, N_EXPERTS):
        m = jnp.maximum(m, l[e])
    ex = [jnp.exp(v - m) for v in l]
    den = ex[0]
    for e in range(1, N_EXPERTS):
        den = den + ex[e]
    p = [v / den for v in ex]
    scores = []
    for gidx in range(N_EXPERT_GROUPS):
        q = p[gidx * EXPERTS_PER_GROUP:(gidx + 1) * EXPERTS_PER_GROUP]
        best = None
        for a in range(EXPERTS_PER_GROUP):
            for c in range(a + 1, EXPERTS_PER_GROUP):
                pair = q[a] + q[c]
                best = pair if best is None else jnp.maximum(best, pair)
        scores.append(best)
    bs = scores[0]
    bg = jnp.zeros_like(bs)
    for gidx in range(1, N_EXPERT_GROUPS):
        better = scores[gidx] > bs
        bs = jnp.where(better, scores[gidx], bs)
        bg = jnp.where(better, float(gidx), bg)
    q = []
    for a in range(EXPERTS_PER_GROUP):
        v = p[a]
        for gidx in range(1, N_EXPERT_GROUPS):
            v = jnp.where(bg == float(gidx), p[gidx * EXPERTS_PER_GROUP + a], v)
        q.append(v)
    p1 = q[0]
    i1 = jnp.zeros_like(p1)
    for a in range(1, EXPERTS_PER_GROUP):
        better = q[a] > p1
        p1 = jnp.where(better, q[a], p1)
        i1 = jnp.where(better, float(a), i1)
    p2 = jnp.full_like(p1, -1.0)
    i2 = jnp.zeros_like(p1)
    for a in range(EXPERTS_PER_GROUP):
        better = (q[a] > p2) & (i1 != float(a))
        p2 = jnp.where(better, q[a], p2)
        i2 = jnp.where(better, float(a), i2)
    tot = p1 + p2
    base = bg * float(EXPERTS_PER_GROUP)
    return base + i1, base + i2, p1 / tot, p2 / tot


def _merge_kernel(xa_ref, xb_ref, u_ref, yf_ref, yb_ref, bp_ref, gate_ref, mod_ref, d_ref, pmt_ref,
                  wav_ref, wag_ref, wb_ref, wo_ref, gf_ref, rwh_ref, rwl_ref, rb_ref,
                  xo_ref, h2_ref, r_ref, c_ref, cnt_ref, wa_bf, wb_bf, wo_bf, *, n_ctx_tiles):
    @pl.when(pl.program_id(0) == 0)
    def _():
        wa_bf[:, :D_MODEL] = wav_ref[...].astype(BF16)
        wa_bf[:, D_MODEL:] = wag_ref[...].astype(BF16)
        wb_bf[...] = wb_ref[...].astype(BF16)
        wo_bf[...] = wo_ref[...].astype(BF16)

    ya = jax.nn.gelu(yf_ref[...].astype(F32) + yb_ref[...].astype(F32)
                     + d_ref[...] * u_ref[...].astype(F32), approximate=True)
    ya = jnp.dot(pmt_ref[...], ya.astype(BF16), preferred_element_type=F32).astype(BF16)
    av = jnp.dot(ya, wa_bf[...], preferred_element_type=F32)
    a_out = av[:, :D_MODEL] * jax.nn.sigmoid(av[:, D_MODEL:])
    b_out = jnp.dot(bp_ref[...], wb_bf[...], preferred_element_type=F32)
    gate = gate_ref[...].astype(F32)
    merged = gate[:, :D_MODEL] * a_out + gate[:, D_MODEL:] * b_out
    mx = jnp.dot(merged.astype(BF16), wo_bf[...], preferred_element_type=F32)
    x = jnp.where(pl.program_id(0) < n_ctx_tiles, xa_ref[...], xb_ref[...])
    xn = x + mod_ref[2:3, :] * mx
    xo_ref[...] = xn
    h2 = _norm_mod(xn, gf_ref[...], mod_ref[3:4, :], mod_ref[4:5, :])
    h2_ref[...] = h2
    h_hi, h_lo = _split_bf16(h2)
    nt_dims = (((1,), (1,)), ((), ()))
    lt = lax.dot_general(rwh_ref[...], h_hi, nt_dims, preferred_element_type=F32)
    lt += lax.dot_general(rwh_ref[...], h_lo, nt_dims, preferred_element_type=F32)
    lt += lax.dot_general(rwl_ref[...], h_hi, nt_dims, preferred_element_type=F32)
    lt = lt + rb_ref[...]
    e1, e2, w1, w2 = _route(lt)
    @pl.when(pl.program_id(0) == 0)
    def _():
        cnt_ref[...] = jnp.zeros_like(cnt_ref)

    eid = lax.broadcasted_iota(jnp.int32, (N_EXPERTS, TM), 0).astype(F32)
    hit1 = eid == e1
    hit2 = eid == e2
    onehot = jnp.where(hit1 | hit2, 1.0, 0.0)
    before = lax.broadcasted_iota(jnp.int32, (TM, TM), 0) < lax.broadcasted_iota(jnp.int32, (TM, TM), 1)
    excl = jnp.dot(onehot.astype(BF16), jnp.where(before, 1.0, 0.0).astype(BF16),
                   preferred_element_type=F32)
    seen = excl + cnt_ref[:, 0:1]
    r1 = jnp.sum(jnp.where(hit1, seen, 0.0), axis=0, keepdims=True)
    r2 = jnp.sum(jnp.where(hit2, seen, 0.0), axis=0, keepdims=True)
    cnt_ref[...] = cnt_ref[...] + jnp.sum(onehot, axis=1, keepdims=True)
    c_ref[...] = cnt_ref[...]
    zero = jnp.zeros_like(e1)
    r_ref[...] = jnp.concatenate([e1, e2, w1, w2, r1, r2, zero, zero], axis=0)


def _merge(rows, n, u, yf, yb, bp, gates, mod, d_skip, pmt, w_a_val, w_a_gate, w_b_out, w_out, layer,
           gf, rwh, rwl, rb, n_ctx_tiles):
    nt = n // TM
    kind = lambda i: (jnp.where(i < n_ctx_tiles, 0, 1), 0, 0)
    tile = lambda w: pl.BlockSpec((TM, w), lambda i: (i, 0))
    full = lambda a: pl.BlockSpec(a.shape, lambda i: (0,) * a.ndim)
    lay = lambda a: pl.BlockSpec((None,) + a.shape[1:], lambda i: (layer, 0, 0))
    return pl.pallas_call(
        functools.partial(_merge_kernel, n_ctx_tiles=n_ctx_tiles),
        grid=(nt,),
        in_specs=[
            *rows[1], tile(SSM_WIDTH), tile(SSM_WIDTH), tile(SSM_WIDTH), tile(CONV_WIDTH),
            tile(2 * D_MODEL),
            pl.BlockSpec((None, N_MOD, D_MODEL), kind),
            full(d_skip), full(pmt), lay(w_a_val), lay(w_a_gate), lay(w_b_out), lay(w_out), full(gf),
            full(rwh), full(rwl), full(rb),
        ],
        out_specs=[
            tile(D_MODEL), tile(D_MODEL),
            pl.BlockSpec((SUBLANES, TM), lambda i: (0, i)),
            pl.BlockSpec((N_EXPERTS, LANES), lambda i: (0, 0)),
        ],
        out_shape=[
            jax.ShapeDtypeStruct((n, D_MODEL), F32),
            jax.ShapeDtypeStruct((n, D_MODEL), F32),
            jax.ShapeDtypeStruct((SUBLANES, n), F32),
            jax.ShapeDtypeStruct((N_EXPERTS, LANES), F32),
        ],
        scratch_shapes=[
            pltpu.VMEM((N_EXPERTS, LANES), F32),
            pltpu.VMEM((SSM_WIDTH, 2 * D_MODEL), BF16),
            pltpu.VMEM((CONV_WIDTH, D_MODEL), BF16),
            pltpu.VMEM((D_MODEL, D_MODEL), BF16),
        ],
        compiler_params=_cparams(("arbitrary",)),
        name="merge",
    )(*rows[0], u, yf, yb, bp, gates, mod, d_skip, pmt, w_a_val, w_a_gate, w_b_out, w_out, gf, rwh, rwl, rb)


TMM = 256


def _row_copy(src, i, dst, j, sem):
    return pltpu.make_async_copy(src.at[pl.ds(i, 1)], dst.at[pl.ds(j, 1)], sem)


def _invert_kernel(s1_ref, s2_ref, pad_lo_ref, pad_hi_ref, inv_ref, *, n):
    def clear(j, carry):
        inv_ref[j] = 0
        return carry

    for e in range(N_EXPERTS):
        lax.fori_loop(pad_lo_ref[e], pad_hi_ref[e], clear, 0)
    lax.fori_loop(pad_hi_ref[N_EXPERTS - 1], inv_ref.shape[0], clear, 0)

    def put(t, carry):
        inv_ref[s1_ref[t]] = t
        inv_ref[s2_ref[t]] = t
        return carry

    lax.fori_loop(0, n, put, 0, unroll=8)


def _invert(s1, s2, pad_lo, pad_hi, ns):
    return pl.pallas_call(
        functools.partial(_invert_kernel, n=s1.shape[0]),
        grid_spec=pltpu.PrefetchScalarGridSpec(
            num_scalar_prefetch=4,
            grid=(1,),
            in_specs=[],
            out_specs=pl.BlockSpec(memory_space=pltpu.SMEM),
        ),
        out_shape=jax.ShapeDtypeStruct((ns,), jnp.int32),
        compiler_params=_cparams(("arbitrary",)),
        name="moe_invert",
    )(s1, s2, pad_lo, pad_hi)


def _ffn_kernel(te_ref, nu_ref, inv_ref, h_hbm, wg_ref, wu_ref, wd_ref, y_ref, xbuf0, xbuf1, sem,
                wg_bf, wu_bf, wd_bf):
    i = pl.program_id(0)
    nu = nu_ref[0]
    xbuf = (xbuf0, xbuf1)

    def issue(tile, slot, rows=range(TMM)):
        for r in rows:
            _row_copy(h_hbm, inv_ref[tile * TMM + r], xbuf[slot], r, sem.at[slot]).start()

    def drain(slot):
        for r in range(TMM):
            _row_copy(h_hbm, 0, xbuf[slot], 0, sem.at[slot]).wait()

    @pl.when((i == 0) & (nu > 0))
    def _():
        issue(0, 0)

    @pl.when((i < nu) & ((i == 0) | (te_ref[i] != te_ref[jnp.maximum(i - 1, 0)])))
    def _():
        wg_bf[...] = wg_ref[...].astype(BF16)
        wu_bf[...] = wu_ref[...].astype(BF16)
        wd_bf[...] = wd_ref[...].astype(BF16)

    def step(slot):
        nxt = jnp.minimum(i + 1, nu - 1)
        issue(nxt, 1 - slot, range(0, TMM // 2))
        drain(slot)
        issue(nxt, 1 - slot, range(TMM // 2, TMM))
        x = xbuf[slot][...].astype(BF16)
        hid = jnp.dot(x, wg_bf[...], preferred_element_type=F32)
        hid = hid * jax.nn.sigmoid(hid) * jnp.dot(x, wu_bf[...], preferred_element_type=F32)
        y_ref[...] = jnp.dot(hid.astype(BF16), wd_bf[...], preferred_element_type=F32)

        @pl.when(i == nu - 1)
        def _():
            drain(1 - slot)

    for slot in range(2):
        @pl.when((i % 2 == slot) & (i < nu))
        def _():
            step(slot)

    @pl.when(i >= nu)
    def _():
        y_ref[...] = jnp.zeros_like(y_ref)


def _ffn(h2, inv, tile_expert, n_used, wg, wu, wd, layer):
    ns = inv.shape[0]
    wspec = lambda *shape: pl.BlockSpec((None, None) + shape, lambda i, te, nu, iv: (layer, te[i], 0, 0))
    return pl.pallas_call(
        _ffn_kernel,
        grid_spec=pltpu.PrefetchScalarGridSpec(
            num_scalar_prefetch=3,
            grid=(ns // TMM,),
            in_specs=[
                pl.BlockSpec(memory_space=pl.ANY),
                wspec(D_MODEL, D_EXPERT), wspec(D_MODEL, D_EXPERT), wspec(D_EXPERT, D_MODEL),
            ],
            out_specs=pl.BlockSpec((TMM, D_MODEL), lambda i, te, nu, iv: (i, 0)),
            scratch_shapes=[pltpu.VMEM((TMM, D_MODEL), F32), pltpu.VMEM((TMM, D_MODEL), F32),
                            pltpu.SemaphoreType.DMA((2,)),
                            pltpu.VMEM((D_MODEL, D_EXPERT), BF16), pltpu.VMEM((D_MODEL, D_EXPERT), BF16),
                            pltpu.VMEM((D_EXPERT, D_MODEL), BF16)],
        ),
        out_shape=jax.ShapeDtypeStruct((ns, D_MODEL), F32),
        compiler_params=_cparams(("arbitrary",)),
        name="moe_ffn",
    )(tile_expert, n_used, inv, h2, wg, wu, wd)


def _combine_kernel(s1_ref, s2_ref, x_ref, rt_ref, mod_ref, g_ref, ys_hbm, o_ref, buf, sem, *,
                    first_tile, final):
    i = pl.program_id(0)
    nt = pl.num_programs(0)

    def issue(tile, slot):
        for r in range(TM):
            t = (first_tile + tile) * TM + r
            _row_copy(ys_hbm, s1_ref[t], buf.at[slot], r, sem.at[slot]).start()
            _row_copy(ys_hbm, s2_ref[t], buf.at[slot], TM + r, sem.at[slot]).start()

    def drain(slot):
        for r in range(2 * TM):
            _row_copy(ys_hbm, 0, buf.at[slot], 0, sem.at[slot]).wait()

    @pl.when(i == 0)
    def _():
        issue(0, 0)

    def step(slot):
        issue(jnp.minimum(i + 1, nt - 1), 1 - slot)
        drain(slot)
        rt = rt_ref[...]
        y = rt[:, 2:3] * buf[slot, 0:TM, :] + rt[:, 3:4] * buf[slot, TM:2 * TM, :]
        out = x_ref[...] + mod_ref[5:6, :] * y
        if final:
            out = out * lax.rsqrt(jnp.mean(out * out, axis=-1, keepdims=True) + NORM_EPS) * g_ref[...]
        o_ref[...] = out

        @pl.when(i == nt - 1)
        def _():
            drain(1 - slot)

    for slot in range(2):
        @pl.when(i % 2 == slot)
        def _():
            step(slot)


def _combine(xn, rt, mod, g, ys, s1, s2, n_ctx_tiles, final):
    n = xn.shape[0]
    first = n_ctx_tiles if final else 0
    tiles = n // TM - first
    kind = lambda i, a, b: (jnp.where(i + first < n_ctx_tiles, 0, 1), 0, 0)
    return pl.pallas_call(
        functools.partial(_combine_kernel, first_tile=first, final=final),
        grid_spec=pltpu.PrefetchScalarGridSpec(
            num_scalar_prefetch=2,
            grid=(tiles,),
            in_specs=[
                pl.BlockSpec((TM, D_MODEL), lambda i, a, b: (i + first, 0)),
                pl.BlockSpec((TM, SUBLANES), lambda i, a, b: (i + first, 0)),
                pl.BlockSpec((None, N_MOD, D_MODEL), kind),
                pl.BlockSpec((1, D_MODEL), lambda i, a, b: (0, 0)),
                pl.BlockSpec(memory_space=pl.ANY),
            ],
            out_specs=pl.BlockSpec((TM, D_MODEL), lambda i, a, b: (i, 0)),
            scratch_shapes=[
                pltpu.VMEM((2, 2 * TM, D_MODEL), F32),
                pltpu.SemaphoreType.DMA((2,)),
            ],
        ),
        out_shape=jax.ShapeDtypeStruct((tiles * TM, D_MODEL), F32),
        compiler_params=_cparams(("arbitrary",)),
        name="moe_combine",
    )(s1, s2, xn, rt, mod, g, ys)


def _pre_gather_kernel(s1_ref, s2_ref, x_ref, rt_ref, modp_ref, ys_hbm, mod_ref, g_ref, w_ref, pm_ref,
                       xc_ref, u_ref, z_ref, gate_ref, buf, sem, wbf_ref):
    i = pl.program_id(0)
    nt = pl.num_programs(0)

    @pl.when(i == 0)
    def _():
        wbf_ref[...] = w_ref[...].astype(BF16)

    def issue(tile, slot, rows=range(TM)):
        for r in rows:
            t = tile * TM + r
            _row_copy(ys_hbm, s1_ref[t], buf.at[slot], r, sem.at[slot]).start()
            _row_copy(ys_hbm, s2_ref[t], buf.at[slot], TM + r, sem.at[slot]).start()

    def drain(slot):
        for r in range(2 * TM):
            _row_copy(ys_hbm, 0, buf.at[slot], 0, sem.at[slot]).wait()

    @pl.when(i == 0)
    def _():
        issue(0, 0)

    def step(slot):
        nxt = jnp.minimum(i + 1, nt - 1)
        issue(nxt, 1 - slot, range(0, TM // 2))
        drain(slot)
        issue(nxt, 1 - slot, range(TM // 2, TM))
        rt = rt_ref[...]
        y = rt[:, 2:3] * buf[slot, 0:TM, :] + rt[:, 3:4] * buf[slot, TM:2 * TM, :]
        x = x_ref[...] + modp_ref[5:6, :] * y
        xc_ref[...] = x
        _pre_project(x, mod_ref, g_ref, wbf_ref, pm_ref, u_ref, z_ref, gate_ref)

        @pl.when(i == nt - 1)
        def _():
            drain(1 - slot)

    for slot in range(2):
        @pl.when(i % 2 == slot)
        def _():
            step(slot)


def _pre_gather(pending, mod, g, w_in, layer, pm, n_ctx_tiles):
    xn, rt, mod_prev, ys, s1, s2 = pending
    n = xn.shape[0]
    kind = lambda i, a, b: (jnp.where(i < n_ctx_tiles, 0, 1), 0, 0)
    tile = lambda w: pl.BlockSpec((TM, w), lambda i, a, b: (i, 0))
    return pl.pallas_call(
        _pre_gather_kernel,
        grid_spec=pltpu.PrefetchScalarGridSpec(
            num_scalar_prefetch=2,
            grid=(n // TM,),
            in_specs=[
                tile(D_MODEL), tile(SUBLANES),
                pl.BlockSpec((None, N_MOD, D_MODEL), kind),
                pl.BlockSpec(memory_space=pl.ANY),
                pl.BlockSpec((None, N_MOD, D_MODEL), kind),
                pl.BlockSpec((1, D_MODEL), lambda i, a, b: (0, 0)),
                pl.BlockSpec((None, D_MODEL, IN_COLS), lambda i, a, b: (layer, 0, 0),
                             pipeline_mode=pl.Buffered(1)),
                pl.BlockSpec((TM, TM), lambda i, a, b: (0, 0)),
            ],
            out_specs=[tile(D_MODEL), tile(SSM_WIDTH), tile(CONV_WIDTH), tile(2 * D_MODEL)],
            scratch_shapes=[
                pltpu.VMEM((2, 2 * TM, D_MODEL), F32),
                pltpu.SemaphoreType.DMA((2,)),
                pltpu.VMEM((D_MODEL, IN_COLS), BF16),
            ],
        ),
        out_shape=[
            jax.ShapeDtypeStruct((n, D_MODEL), F32),
            jax.ShapeDtypeStruct((n, SSM_WIDTH), BF16),
            jax.ShapeDtypeStruct((n, CONV_WIDTH), F32),
            jax.ShapeDtypeStruct((n, 2 * D_MODEL), BF16),
        ],
        compiler_params=_cparams(("arbitrary",)),
        name="pre_gather",
    )(s1, s2, xn, rt, mod_prev, ys, mod, g, w_in, pm)


def _moe(xn, h2, r, counts, mod, wg, wu, wd, layer, final_g, n_ctx_tiles, final):
    n = xn.shape[0]
    ns = 2 * n + N_EXPERTS * TMM
    cnt = counts[:, 0].astype(jnp.int32)
    padded = (cnt + TMM - 1) // TMM * TMM
    ends = jnp.cumsum(padded)
    off = ends - padded
    experts = jnp.arange(N_EXPERTS, dtype=jnp.int32)

    def slot(e_row, rank_row):
        hit = e_row.astype(jnp.int32)[:, None] == experts[None, :]
        return jnp.sum(jnp.where(hit, off[None, :], 0), axis=1) + rank_row.astype(jnp.int32)

    s1 = slot(r[0], r[4])
    s2 = slot(r[1], r[5])
    tile_start = jnp.arange(ns // TMM, dtype=jnp.int32) * TMM
    tile_expert = jnp.minimum(jnp.sum(ends[None, :] <= tile_start[:, None], axis=1),
                              N_EXPERTS - 1).astype(jnp.int32)
    n_used = (ends[-1:] // TMM).astype(jnp.int32)
    inv = _invert(s1, s2, off + cnt, ends, ns)
    ys = _ffn(h2, inv, tile_expert, n_used, wg, wu, wd, layer)
    if final:
        return _combine(xn, r.T, mod, final_g, ys, s1, s2, n_ctx_tiles, final)
    return xn, r.T, mod, ys, s1, s2


def kernel(x, c, ctx, c_ctx, ada_w, ada_b, norm_mix_g, norm_ffn_g, w_in, ssm_a_re, ssm_a_im, ssm_log_dt, ssm_b_re, ssm_b_im, ssm_c_re, ssm_c_im, ssm_d, w_a_val, w_a_gate, conv_w, conv_b, conv_ln_g, conv_ln_b, w_b_out, w_out, router_w, router_b, w_exp_gate, w_exp_up, w_exp_down, final_norm_g):
    depth = ada_w.shape[0]
    bsz, seq, _ = x.shape
    n_ctx = ctx.shape[1]
    assert bsz == 1 and n_ctx == TM and seq % TM == 0 and seq % GRID_W == 0
    n_ctx_tiles = n_ctx // TM
    n = n_ctx + seq

    rows = _row_sources(ctx[0], x[0], 0, n_ctx_tiles)
    cc = jnp.zeros((SUBLANES, D_MODEL), F32).at[0].set(c_ctx).at[1].set(c[0])
    mods = _mods(cc, ada_w, ada_b)[:, :2].reshape(depth, 2, N_MOD, D_MODEL)

    rwt = router_w.T
    rwh, rwl = _split_bf16(rwt)
    rb = router_b.reshape(N_EXPERTS, 1)
    pm, pmt = _ssm_perm()

    pending = None
    for l in range(depth):
        last = l == depth - 1
        g_mix = norm_mix_g[l].reshape(1, -1)
        if pending is None:
            u, z, gates = _pre(rows, n, mods[l], g_mix, w_in, l, pm, n_ctx_tiles)
        else:
            xc, u, z, gates = _pre_gather(pending, mods[l], g_mix, w_in, l, pm, n_ctx_tiles)
            rows = _row_sources(xc, xc, n_ctx_tiles, n_ctx_tiles)
        prm =[_ssm_params(ssm_a_re[l, k], ssm_a_im[l, k], ssm_log_dt[l, k], ssm_b_re[l, k],
                           ssm_b_im[l, k], ssm_c_re[l, k], ssm_c_im[l, k]) for k in range(2)]
        ys = _ssm(u, *[jnp.stack(pair) for pair in zip(*prm)], n_ctx_tiles)
        vec = lambda a: a.reshape(1, -1)
        conv_args = (conv_w[l], vec(conv_b[l]), vec(conv_ln_g[l]), vec(conv_ln_b[l]))
        if l % 2 == 0:
            bp = _conv_rows(z, *conv_args, n_ctx_tiles)
        else:
            assert last, "column conv over the context prefix is not needed by the last layer only"
            bp = _conv_cols(z, *conv_args, n_ctx)
        xn, h2, r, counts = _merge(rows, n, u, ys[0], ys[1], bp, gates, mods[l], vec(ssm_d[l]), pmt,
                                   w_a_val, w_a_gate, w_b_out, w_out, l, vec(norm_ffn_g[l]),
                                   rwh, rwl, rb, n_ctx_tiles)
        pending = _moe(xn, h2, r, counts, mods[l], w_exp_gate, w_exp_up, w_exp_down, l,
                       vec(final_norm_g), n_ctx_tiles, final=last)
    return pending[None]
```

```python
import functools
import math

import jax
import jax.numpy as jnp
from jax import lax
from jax.experimental import pallas as pl
from jax.experimental.pallas import tpu as pltpu

F32 = jnp.float32
BF16 = jnp.bfloat16

D_MODEL = 1024
GRID_W = 64
SSM_WIDTH = 512
SSM_GROUP = 16
SSM_GROUPS = SSM_WIDTH // SSM_GROUP
SSM_STATE = 64
CONV_WIDTH = 512
CONV_SIZE = 31
CONV_HALF = CONV_SIZE // 2
IN_COLS = SSM_WIDTH + 2 * CONV_WIDTH + 2 * D_MODEL
N_EXPERTS = 16
N_EXPERT_GROUPS = 4
EXPERTS_PER_GROUP = N_EXPERTS // N_EXPERT_GROUPS
D_EXPERT = 512
N_MOD = 6
NORM_EPS = 1e-6

TM = 256
LANES = 128
SUBLANES = 8
SEG = TM // SUBLANES
KB = SSM_WIDTH // LANES
GPB = LANES // SSM_GROUP
SB = GPB * SSM_STATE
PAD = 16
CHUNK = 32
VMEM_LIMIT = 56 * 1024 * 1024


def _cparams(sem):
    return pltpu.CompilerParams(dimension_semantics=sem, vmem_limit_bytes=VMEM_LIMIT)


def _split_bf16(a):
    hi = a.astype(BF16)
    lo = (a - hi.astype(F32)).astype(BF16)
    return hi, lo


def _mods_kernel(c_ref, w_ref, b_ref, o_ref):
    s = c_ref[...]
    s = s * jax.nn.sigmoid(s)
    s_hi, s_lo = _split_bf16(s)
    w_hi, w_lo = _split_bf16(w_ref[...])
    acc = jnp.dot(s_hi, w_hi, preferred_element_type=F32)
    acc += jnp.dot(s_lo, w_hi, preferred_element_type=F32)
    acc += jnp.dot(s_hi, w_lo, preferred_element_type=F32)
    o_ref[...] = acc + b_ref[...]


def _mods(cc, ada_w, ada_b):
    depth = ada_w.shape[0]
    cols = ada_w.shape[2]
    cb = 1536
    return pl.pallas_call(
        _mods_kernel,
        grid=(depth, cols // cb),
        in_specs=[
            pl.BlockSpec((SUBLANES, D_MODEL), lambda l, j: (0, 0)),
            pl.BlockSpec((None, D_MODEL, cb), lambda l, j: (l, 0, j)),
            pl.BlockSpec((None, 1, cb), lambda l, j: (l, 0, j)),
        ],
        out_specs=pl.BlockSpec((None, SUBLANES, cb), lambda l, j: (l, 0, j)),
        out_shape=jax.ShapeDtypeStruct((depth, SUBLANES, cols), F32),
        compiler_params=_cparams(("arbitrary", "arbitrary")),
        name="mods",
    )(cc, ada_w, ada_b.reshape(depth, 1, cols))


def _norm_mod(x, g, shift, scale):
    y = x * lax.rsqrt(jnp.mean(x * x, axis=-1, keepdims=True) + NORM_EPS) * g
    return y * (1.0 + scale) + shift


def _pre_kernel(xa_ref, xb_ref, mod_ref, g_ref, w_ref, pm_ref, u_ref, z_ref, gate_ref, wbf_ref, *,
                n_ctx_tiles):
    @pl.when(pl.program_id(0) == 0)
    def _():
        wbf_ref[...] = w_ref[...].astype(BF16)

    x = jnp.where(pl.program_id(0) < n_ctx_tiles, xa_ref[...], xb_ref[...])
    _pre_project(x, mod_ref, g_ref, wbf_ref, pm_ref, u_ref, z_ref, gate_ref)


def _pre_project(x, mod_ref, g_ref, wbf_ref, pm_ref, u_ref, z_ref, gate_ref):
    h = _norm_mod(x, g_ref[...], mod_ref[0:1, :], mod_ref[1:2, :])
    p = jnp.dot(h.astype(BF16), wbf_ref[...], preferred_element_type=F32)
    u_ref[...] = jnp.dot(pm_ref[...], p[:, :SSM_WIDTH].astype(BF16),
                         preferred_element_type=F32).astype(BF16)
    v1 = p[:, SSM_WIDTH:SSM_WIDTH + CONV_WIDTH]
    v2 = p[:, SSM_WIDTH + CONV_WIDTH:SSM_WIDTH + 2 * CONV_WIDTH]
    z_ref[...] = v1 * jax.nn.sigmoid(v2)
    gate_ref[...] = jax.nn.sigmoid(p[:, SSM_WIDTH + 2 * CONV_WIDTH:]).astype(BF16)


def _row_sources(rows_ctx, rows_lat, lat_offset, n_ctx_tiles):
    spec_ctx = pl.BlockSpec((TM, D_MODEL), lambda i, *_: (jnp.minimum(i, n_ctx_tiles - 1), 0))
    spec_lat = pl.BlockSpec((TM, D_MODEL),
                            lambda i, *_: (jnp.maximum(i - n_ctx_tiles, 0) + lat_offset, 0))
    return (rows_ctx, rows_lat), (spec_ctx, spec_lat)


def _pre(rows, n, mod, g, w_in, layer, pm, n_ctx_tiles):
    nt = n // TM
    kind = lambda i: (jnp.where(i < n_ctx_tiles, 0, 1), 0, 0)
    return pl.pallas_call(
        functools.partial(_pre_kernel, n_ctx_tiles=n_ctx_tiles),
        grid=(nt,),
        in_specs=[
            *rows[1],
            pl.BlockSpec((None, N_MOD, D_MODEL), kind),
            pl.BlockSpec((1, D_MODEL), lambda i: (0, 0)),
            pl.BlockSpec((None, D_MODEL, IN_COLS), lambda i: (layer, 0, 0), pipeline_mode=pl.Buffered(1)),
            pl.BlockSpec((TM, TM), lambda i: (0, 0)),
        ],
        out_specs=[
            pl.BlockSpec((TM, SSM_WIDTH), lambda i: (i, 0)),
            pl.BlockSpec((TM, CONV_WIDTH), lambda i: (i, 0)),
            pl.BlockSpec((TM, 2 * D_MODEL), lambda i: (i, 0)),
        ],
        out_shape=[
            jax.ShapeDtypeStruct((n, SSM_WIDTH), BF16),
            jax.ShapeDtypeStruct((n, CONV_WIDTH), F32),
            jax.ShapeDtypeStruct((n, 2 * D_MODEL), BF16),
        ],
        scratch_shapes=[pltpu.VMEM((D_MODEL, IN_COLS), BF16)],
        compiler_params=_cparams(("arbitrary",)),
        name="pre",
    )(*rows[0], mod, g, w_in, pm)


def _ssm_kernel(uf_ref, ub_ref, bw_ref, cw_ref, ar_ref, ai_ref, qr_ref, qi_ref,
                yf_ref, yb_ref, cr_ref, ci_ref, sf_ref, sb_ref, hf_ref, hb_ref):
    @pl.when(pl.program_id(0) == 0)
    def _():
        cr_ref[...] = jnp.zeros_like(cr_ref)
        ci_ref[...] = jnp.zeros_like(ci_ref)

    dirs = (0, 1)
    u_refs = (uf_ref, ub_ref)
    y_refs = (yf_ref, yb_ref)
    s_refs = (sf_ref, sb_ref)
    h_refs = (hf_ref, hb_ref)

    def group(d, i):
        return SEG - 1 - i if d == 1 else i

    def block(k):
        lanes = slice(LANES * k, LANES * (k + 1))
        for d in dirs:
            s_refs[d][...] = jnp.dot(u_refs[d][:, lanes], bw_ref[d, k], preferred_element_type=F32)

        ar = [ar_ref[d, k] for d in dirs]
        ai = [ai_ref[d, k] for d in dirs]

        def advance(carry, i):
            out = []
            for d in dirs:
                rows = pl.ds(pl.multiple_of(group(d, i) * SUBLANES, SUBLANES), SUBLANES)
                hr_d, hi_d = carry[2 * d], carry[2 * d + 1]
                out.append(ar[d] * hr_d - ai[d] * hi_d + s_refs[d][rows, 0:SB])
                out.append(ar[d] * hi_d + ai[d] * hr_d + s_refs[d][rows, SB:2 * SB])
            return tuple(out)

        zero = jnp.zeros((SUBLANES, SB), F32)
        ends = lax.fori_loop(0, SEG, lambda i, c: advance(c, i), (zero,) * 4, unroll=2)
        hr = [ends[0], ends[2]]
        hi = [ends[1], ends[3]]

        hpr = [None, None]
        hpi = [None, None]
        for d in dirs:
            asr = qr_ref[d, k, 0:1, :]
            asi = qi_ref[d, k, 0:1, :]
            pr_ = cr_ref[d, k]
            pi_ = ci_ref[d, k]
            prev_r = [None] * SUBLANES
            prev_i = [None] * SUBLANES
            order = range(SUBLANES - 1, -1, -1) if d == 1 else range(SUBLANES)
            for s in order:
                prev_r[s] = pr_
                prev_i[s] = pi_
                er = hr[d][s:s + 1, :]
                ei = hi[d][s:s + 1, :]
                pr_, pi_ = er + asr * pr_ - asi * pi_, ei + asr * pi_ + asi * pr_
            cr_ref[d, k] = pr_
            ci_ref[d, k] = pi_
            hpr[d] = jnp.concatenate(prev_r, axis=0)
            hpi[d] = jnp.concatenate(prev_i, axis=0)

        def rescan(j, carry):
            first = advance(carry, 2 * j)
            second = advance(first, 2 * j + 1)
            for d in dirs:
                lo, hi_ = (first, second) if d == 0 else (second, first)
                pair = j if d == 0 else SEG // 2 - 1 - j
                rows = pl.ds(pl.multiple_of(pair * 2 * SUBLANES, 2 * SUBLANES), 2 * SUBLANES)
                h_refs[d][rows, 0:SB] = jnp.concatenate([lo[2 * d], hi_[2 * d]], axis=0).astype(BF16)
                h_refs[d][rows, SB:2 * SB] = jnp.concatenate(
                    [lo[2 * d + 1], hi_[2 * d + 1]], axis=0).astype(BF16)
            return second

        lax.fori_loop(0, SEG // 2, rescan, (hpr[0], hpi[0], hpr[1], hpi[1]))

        for d in dirs:
            y_refs[d][:, lanes] = jnp.dot(h_refs[d][...], cw_ref[d, k],
                                          preferred_element_type=F32).astype(BF16)

    for k in range(KB):
        block(k)


def _ssm(u, bw, cw, ar, ai, qr, qi, n_ctx_tiles):
    n = u.shape[0]
    nt = n // TM

    def fwd(t):
        return (t, 0)

    def bwd(t):
        return (jnp.where(t < n_ctx_tiles, n_ctx_tiles - 1 - t, nt - 1 - (t - n_ctx_tiles)), 0)

    full = lambda a: pl.BlockSpec(a.shape, lambda t: (0,) * a.ndim)
    return pl.pallas_call(
        _ssm_kernel,
        grid=(nt,),
        in_specs=[
            pl.BlockSpec((TM, SSM_WIDTH), fwd),
            pl.BlockSpec((TM, SSM_WIDTH), bwd),
            full(bw), full(cw), full(ar), full(ai), full(qr), full(qi),
        ],
        out_specs=[pl.BlockSpec((TM, SSM_WIDTH), fwd), pl.BlockSpec((TM, SSM_WIDTH), bwd)],
        out_shape=[jax.ShapeDtypeStruct((n, SSM_WIDTH), BF16)] * 2,
        scratch_shapes=[
            pltpu.VMEM((2, KB, 1, SB), F32),
            pltpu.VMEM((2, KB, 1, SB), F32),
            pltpu.VMEM((TM, 2 * SB), F32),
            pltpu.VMEM((TM, 2 * SB), F32),
            pltpu.VMEM((TM, 2 * SB), BF16),
            pltpu.VMEM((TM, 2 * SB), BF16),
        ],
        compiler_params=_cparams(("arbitrary",)),
        name="ssm",
    )(u, u, bw, cw, ar, ai, qr, qi)


def _ssm_perm():
    p = jnp.arange(TM)
    src = SEG * (p % SUBLANES) + p // SUBLANES
    pm = (src[:, None] == jnp.arange(TM)[None, :]).astype(BF16)
    return pm, pm.T


def _ssm_params(a_re, a_im, log_dt, b_re, b_im, c_re, c_im):
    lam_re = jnp.minimum(a_re.astype(F32), -1e-4)
    lam_im = a_im.astype(F32)
    dt = jnp.exp(log_dt.astype(F32))[:, None]
    mag = jnp.exp(lam_re * dt)
    abar_re = mag * jnp.cos(lam_im * dt)
    abar_im = mag * jnp.sin(lam_im * dt)
    n_re, n_im = abar_re - 1.0, abar_im
    den = lam_re * lam_re + lam_im * lam_im
    coef_re = (n_re * lam_re + n_im * lam_im) / den
    coef_im = (n_im * lam_re - n_re * lam_im) / den
    br, bi = b_re.astype(F32), b_im.astype(F32)
    bbar_re = coef_re[..., None] * br - coef_im[..., None] * bi
    bbar_im = coef_re[..., None] * bi + coef_im[..., None] * br
    eye = jnp.eye(GPB, dtype=F32)

    def in_block(b):
        b = b.reshape(KB, GPB, SSM_STATE, SSM_GROUP)
        return jnp.einsum('kgpc,gh->kgchp', b, eye).reshape(KB, LANES, SB)

    def out_block(c):
        c = c.reshape(KB, GPB, SSM_GROUP, SSM_STATE)
        return jnp.einsum('kgcp,gh->kgphc', c, eye).reshape(KB, SB, LANES)

    bw = jnp.concatenate([in_block(bbar_re), in_block(bbar_im)], axis=2).astype(BF16)
    cw = jnp.concatenate([out_block(c_re.astype(F32)), -out_block(c_im.astype(F32))], axis=1).astype(BF16)
    qr, qi = abar_re, abar_im
    for _ in range(SEG - 1):
        qr, qi = qr * abar_re - qi * abar_im, qr * abar_im + qi * abar_re
    rep = lambda a: jnp.broadcast_to(a.reshape(KB, 1, SB), (KB, SUBLANES, SB))
    return bw, cw, rep(abar_re), rep(abar_im), rep(qr), rep(qi)


def _ln_silu(acc, g, b):
    mu = jnp.mean(acc, axis=-1, keepdims=True)
    xc = acc - mu
    y = xc * lax.rsqrt(jnp.mean(xc * xc, axis=-1, keepdims=True) + NORM_EPS) * g + b
    return y * jax.nn.sigmoid(y)


def _conv_rows_kernel(z_ref, w_ref, cb_ref, g_ref, b_ref, o_ref, zp_ref, *, n_ctx_tiles):
    i = pl.program_id(0)
    w = w_ref[...]
    cb = cb_ref[...]
    g = g_ref[...]
    b = b_ref[...]

    def run(seq):
        nseq = TM // seq
        stride = seq + 2 * PAD
        total = nseq * stride
        zeros = jnp.zeros((PAD, CONV_WIDTH), F32)
        for a in range(nseq):
            zp_ref[0, stride * a:stride * a + PAD, :] = zeros
            zp_ref[0, stride * a + PAD:stride * a + PAD + seq, :] = z_ref[seq * a:seq * (a + 1), :]
            zp_ref[0, stride * a + PAD + seq:stride * (a + 1), :] = zeros
        for k in range(1, SUBLANES):
            for r0 in range(0, total - SUBLANES, CHUNK):
                rows = min(CHUNK, total - SUBLANES - r0)
                zp_ref[k, r0:r0 + rows, :] = zp_ref[0, r0 + k:r0 + k + rows, :]
        for a in range(nseq):
            for r0 in range(0, seq, CHUNK):
                acc = jnp.zeros((CHUNK, CONV_WIDTH), F32) + cb
                for d in range(CONV_SIZE):
                    off = PAD + d - CONV_HALF
                    start = stride * a + r0 + off // SUBLANES * SUBLANES
                    acc = acc + w[d:d + 1, :] * zp_ref[off % SUBLANES, start:start + CHUNK, :]
                o_ref[seq * a + r0:seq * a + r0 + CHUNK, :] = _ln_silu(acc, g, b).astype(o_ref.dtype)

    @pl.when(i < n_ctx_tiles)
    def _():
        run(TM)

    @pl.when(i >= n_ctx_tiles)
    def _():
        run(GRID_W)


def _conv_rows(z, w, cb, g, b, n_ctx_tiles):
    n = z.shape[0]
    nt = n // TM
    vec = pl.BlockSpec((1, CONV_WIDTH), lambda i: (0, 0))
    return pl.pallas_call(
        functools.partial(_conv_rows_kernel, n_ctx_tiles=n_ctx_tiles),
        grid=(nt,),
        in_specs=[
            pl.BlockSpec((TM, CONV_WIDTH), lambda i: (i, 0)),
            pl.BlockSpec((CONV_SIZE, CONV_WIDTH), lambda i: (0, 0)),
            vec, vec, vec,
        ],
        out_specs=pl.BlockSpec((TM, CONV_WIDTH), lambda i: (i, 0)),
        out_shape=jax.ShapeDtypeStruct((n, CONV_WIDTH), BF16),
        scratch_shapes=[pltpu.VMEM((SUBLANES, TM // GRID_W * (GRID_W + 2 * PAD), CONV_WIDTH), F32)],
        compiler_params=_cparams(("arbitrary",)),
        name="conv_rows",
    )(z, w, cb, g, b)


WB = 16
RB = 16


def _conv_cols_kernel(z_ref, w_ref, cb_ref, g_ref, b_ref, o_ref, zp_ref, *, r_ctx, rows):
    w = w_ref[...]
    cb = cb_ref[...]
    g = g_ref[...]
    b = b_ref[...]
    zeros = jnp.zeros((CONV_HALF, WB, CONV_WIDTH), F32)
    zp_ref[0:CONV_HALF] = zeros
    zp_ref[CONV_HALF + rows:CONV_HALF + rows + CONV_HALF] = zeros
    zp_ref[CONV_HALF:CONV_HALF + rows] = z_ref[r_ctx:r_ctx + rows]
    o_ref[0:r_ctx] = jnp.zeros((r_ctx, WB, CONV_WIDTH), o_ref.dtype)

    def column(h, c):
        hs = slice(SUBLANES * h, SUBLANES * (h + 1))
        ls = slice(LANES * c, LANES * (c + 1))
        wv = [jnp.broadcast_to(w[d:d + 1, ls], (SUBLANES, LANES)) for d in range(CONV_SIZE)]
        bias = jnp.broadcast_to(cb[:, ls], (SUBLANES, LANES))

        def body(blk, carry):
            r0 = blk * RB
            acc = [bias] * RB
            for q in range(RB + CONV_SIZE - 1):
                zq = zp_ref[r0 + q, hs, ls]
                for j in range(RB):
                    d = q - j
                    if 0 <= d < CONV_SIZE:
                        acc[j] = acc[j] + wv[d] * zq
            for j in range(RB):
                zp_ref[r0 + j, hs, ls] = acc[j]
            return carry

        lax.fori_loop(0, rows // RB, body, 0)

    for h in range(WB // SUBLANES):
        for c in range(CONV_WIDTH // LANES):
            column(h, c)

    def norm(blk, carry):
        r0 = blk * SUBLANES
        v = zp_ref[pl.ds(r0, SUBLANES)].reshape(SUBLANES * WB, CONV_WIDTH)
        y = _ln_silu(v, g, b).astype(o_ref.dtype)
        o_ref[pl.ds(r_ctx + r0, SUBLANES)] = y.reshape(SUBLANES, WB, CONV_WIDTH)
        return carry

    lax.fori_loop(0, rows // SUBLANES, norm, 0)


def _conv_cols(z, w, cb, g, b, n_ctx):
    n = z.shape[0]
    r_all = n // GRID_W
    r_ctx = n_ctx // GRID_W
    rows = r_all - r_ctx
    z3 = z.reshape(r_all, GRID_W, CONV_WIDTH)
    vec = pl.BlockSpec((1, CONV_WIDTH), lambda j: (0, 0))
    out = pl.pallas_call(
        functools.partial(_conv_cols_kernel, r_ctx=r_ctx, rows=rows),
        grid=(GRID_W // WB,),
        in_specs=[
            pl.BlockSpec((r_all, WB, CONV_WIDTH), lambda j: (0, j, 0)),
            pl.BlockSpec((CONV_SIZE, CONV_WIDTH), lambda j: (0, 0)),
            vec, vec, vec,
        ],
        out_specs=pl.BlockSpec((r_all, WB, CONV_WIDTH), lambda j: (0, j, 0)),
        out_shape=jax.ShapeDtypeStruct((r_all, GRID_W, CONV_WIDTH), BF16),
        scratch_shapes=[pltpu.VMEM((rows + 2 * CONV_HALF, WB, CONV_WIDTH), F32)],
        compiler_params=_cparams(("arbitrary",)),
        name="conv_cols",
    )(z3, w, cb, g, b)
    return out.reshape(n, CONV_WIDTH)


def _route(lt):
    l = [lt[e:e + 1, :] for e in range(N_EXPERTS)]
    m = l[0]
    for e in range(1, N_EXPERTS):
        m = jnp.maximum(m, l[e])
    ex = [jnp.exp(v - m) for v in l]
    den = ex[0]
    for e in range(1, N_EXPERTS):
        den = den + ex[e]
    p = [v / den for v in ex]
    scores = []
    for gidx in range(N_EXPERT_GROUPS):
        q = p[gidx * EXPERTS_PER_GROUP:(gidx + 1) * EXPERTS_PER_GROUP]
        best = None
        for a in range(EXPERTS_PER_GROUP):
            for c in range(a + 1, EXPERTS_PER_GROUP):
                pair = q[a] + q[c]
                best = pair if best is None else jnp.maximum(best, pair)
        scores.append(best)
    bs = scores[0]
    bg = jnp.zeros_like(bs)
    for gidx in range(1, N_EXPERT_GROUPS):
        better = scores[gidx] > bs
        bs = jnp.where(better, scores[gidx], bs)
        bg = jnp.where(better, float(gidx), bg)
    q = []
    for a in range(EXPERTS_PER_GROUP):
        v = p[a]
        for gidx in range(1, N_EXPERT_GROUPS):
            v = jnp.where(bg == float(gidx), p[gidx * EXPERTS_PER_GROUP + a], v)
        q.append(v)
    p1 = q[0]
    i1 = jnp.zeros_like(p1)
    for a in range(1, EXPERTS_PER_GROUP):
        better = q[a] > p1
        p1 = jnp.where(better, q[a], p1)
        i1 = jnp.where(better, float(a), i1)
    p2 = jnp.full_like(p1, -1.0)
    i2 = jnp.zeros_like(p1)
    for a in range(EXPERTS_PER_GROUP):
        better = (q[a] > p2) & (i1 != float(a))
        p2 = jnp.where(better, q[a], p2)
        i2 = jnp.where(better, float(a), i2)
    tot = p1 + p2
    base = bg * float(EXPERTS_PER_GROUP)
    return base + i1, base + i2, p1 / tot, p2 / tot


def _merge_kernel(xa_ref, xb_ref, u_ref, yf_ref, yb_ref, bp_ref, gate_ref, mod_ref, d_ref, pmt_ref,
                  wav_ref, wag_ref, wb_ref, wo_ref, gf_ref, rwh_ref, rwl_ref, rb_ref,
                  xo_ref, h2_ref, r_ref, c_ref, cnt_ref, wa_bf, wb_bf, wo_bf, *, n_ctx_tiles):
    @pl.when(pl.program_id(0) == 0)
    def _():
        wa_bf[:, :D_MODEL] = wav_ref[...].astype(BF16)
        wa_bf[:, D_MODEL:] = wag_ref[...].astype(BF16)
        wb_bf[...] = wb_ref[...].astype(BF16)
        wo_bf[...] = wo_ref[...].astype(BF16)

    ya = jax.nn.gelu(yf_ref[...].astype(F32) + yb_ref[...].astype(F32)
                     + d_ref[...] * u_ref[...].astype(F32), approximate=True)
    ya = jnp.dot(pmt_ref[...], ya.astype(BF16), preferred_element_type=F32).astype(BF16)
    av = jnp.dot(ya, wa_bf[...], preferred_element_type=F32)
    a_out = av[:, :D_MODEL] * jax.nn.sigmoid(av[:, D_MODEL:])
    b_out = jnp.dot(bp_ref[...], wb_bf[...], preferred_element_type=F32)
    gate = gate_ref[...].astype(F32)
    merged = gate[:, :D_MODEL] * a_out + gate[:, D_MODEL:] * b_out
    mx = jnp.dot(merged.astype(BF16), wo_bf[...], preferred_element_type=F32)
    x = jnp.where(pl.program_id(0) < n_ctx_tiles, xa_ref[...], xb_ref[...])
    xn = x + mod_ref[2:3, :] * mx
    xo_ref[...] = xn
    h2 = _norm_mod(xn, gf_ref[...], mod_ref[3:4, :], mod_ref[4:5, :])
    h2_ref[...] = h2
    h_hi, h_lo = _split_bf16(h2)
    nt_dims = (((1,), (1,)), ((), ()))
    lt = lax.dot_general(rwh_ref[...], h_hi, nt_dims, preferred_element_type=F32)
    lt += lax.dot_general(rwh_ref[...], h_lo, nt_dims, preferred_element_type=F32)
    lt += lax.dot_general(rwl_ref[...], h_hi, nt_dims, preferred_element_type=F32)
    lt = lt + rb_ref[...]
    e1, e2, w1, w2 = _route(lt)
    @pl.when(pl.program_id(0) == 0)
    def _():
        cnt_ref[...] = jnp.zeros_like(cnt_ref)

    eid = lax.broadcasted_iota(jnp.int32, (N_EXPERTS, TM), 0).astype(F32)
    hit1 = eid == e1
    hit2 = eid == e2
    onehot = jnp.where(hit1 | hit2, 1.0, 0.0)
    before = lax.broadcasted_iota(jnp.int32, (TM, TM), 0) < lax.broadcasted_iota(jnp.int32, (TM, TM), 1)
    excl = jnp.dot(onehot.astype(BF16), jnp.where(before, 1.0, 0.0).astype(BF16),
                   preferred_element_type=F32)
    seen = excl + cnt_ref[:, 0:1]
    r1 = jnp.sum(jnp.where(hit1, seen, 0.0), axis=0, keepdims=True)
    r2 = jnp.sum(jnp.where(hit2, seen, 0.0), axis=0, keepdims=True)
    cnt_ref[...] = cnt_ref[...] + jnp.sum(onehot, axis=1, keepdims=True)
    c_ref[...] = cnt_ref[...]
    zero = jnp.zeros_like(e1)
    r_ref[...] = jnp.concatenate([e1, e2, w1, w2, r1, r2, zero, zero], axis=0)


def _merge(rows, n, u, yf, yb, bp, gates, mod, d_skip, pmt, w_a_val, w_a_gate, w_b_out, w_out, layer,
           gf, rwh, rwl, rb, n_ctx_tiles):
    nt = n // TM
    kind = lambda i: (jnp.where(i < n_ctx_tiles, 0, 1), 0, 0)
    tile = lambda w: pl.BlockSpec((TM, w), lambda i: (i, 0))
    full = lambda a: pl.BlockSpec(a.shape, lambda i: (0,) * a.ndim)
    lay = lambda a: pl.BlockSpec((None,) + a.shape[1:], lambda i: (layer, 0, 0))
    return pl.pallas_call(
        functools.partial(_merge_kernel, n_ctx_tiles=n_ctx_tiles),
        grid=(nt,),
        in_specs=[
            *rows[1], tile(SSM_WIDTH), tile(SSM_WIDTH), tile(SSM_WIDTH), tile(CONV_WIDTH),
            tile(2 * D_MODEL),
            pl.BlockSpec((None, N_MOD, D_MODEL), kind),
            full(d_skip), full(pmt), lay(w_a_val), lay(w_a_gate), lay(w_b_out), lay(w_out), full(gf),
            full(rwh), full(rwl), full(rb),
        ],
        out_specs=[
            tile(D_MODEL), tile(D_MODEL),
            pl.BlockSpec((SUBLANES, TM), lambda i: (0, i)),
            pl.BlockSpec((N_EXPERTS, LANES), lambda i: (0, 0)),
        ],
        out_shape=[
            jax.ShapeDtypeStruct((n, D_MODEL), F32),
            jax.ShapeDtypeStruct((n, D_MODEL), F32),
            jax.ShapeDtypeStruct((SUBLANES, n), F32),
            jax.ShapeDtypeStruct((N_EXPERTS, LANES), F32),
        ],
        scratch_shapes=[
            pltpu.VMEM((N_EXPERTS, LANES), F32),
            pltpu.VMEM((SSM_WIDTH, 2 * D_MODEL), BF16),
            pltpu.VMEM((CONV_WIDTH, D_MODEL), BF16),
            pltpu.VMEM((D_MODEL, D_MODEL), BF16),
        ],
        compiler_params=_cparams(("arbitrary",)),
        name="merge",
    )(*rows[0], u, yf, yb, bp, gates, mod, d_skip, pmt, w_a_val, w_a_gate, w_b_out, w_out, gf, rwh, rwl, rb)


TMM = 256


def _row_copy(src, i, dst, j, sem):
    return pltpu.make_async_copy(src.at[pl.ds(i, 1)], dst.at[pl.ds(j, 1)], sem)


def _invert_kernel(s1_ref, s2_ref, pad_lo_ref, pad_hi_ref, inv_ref, *, n):
    def clear(j, carry):
        inv_ref[j] = 0
        return carry

    for e in range(N_EXPERTS):
        lax.fori_loop(pad_lo_ref[e], pad_hi_ref[e], clear, 0)
    lax.fori_loop(pad_hi_ref[N_EXPERTS - 1], inv_ref.shape[0], clear, 0)

    def put(t, carry):
        inv_ref[s1_ref[t]] = t
        inv_ref[s2_ref[t]] = t
        return carry

    lax.fori_loop(0, n, put, 0, unroll=8)


def _invert(s1, s2, pad_lo, pad_hi, ns):
    return pl.pallas_call(
        functools.partial(_invert_kernel, n=s1.shape[0]),
        grid_spec=pltpu.PrefetchScalarGridSpec(
            num_scalar_prefetch=4,
            grid=(1,),
            in_specs=[],
            out_specs=pl.BlockSpec(memory_space=pltpu.SMEM),
        ),
        out_shape=jax.ShapeDtypeStruct((ns,), jnp.int32),
        compiler_params=_cparams(("arbitrary",)),
        name="moe_invert",
    )(s1, s2, pad_lo, pad_hi)


def _ffn_kernel(te_ref, nu_ref, inv_ref, h_hbm, wg_ref, wu_ref, wd_ref, y_ref, xbuf0, xbuf1, sem,
                wg_bf, wu_bf, wd_bf):
    i = pl.program_id(0)
    nu = nu_ref[0]
    xbuf = (xbuf0, xbuf1)

    def issue(tile, slot, rows=range(TMM)):
        for r in rows:
            _row_copy(h_hbm, inv_ref[tile * TMM + r], xbuf[slot], r, sem.at[slot]).start()

    def drain(slot):
        for r in range(TMM):
            _row_copy(h_hbm, 0, xbuf[slot], 0, sem.at[slot]).wait()

    @pl.when((i == 0) & (nu > 0))
    def _():
        issue(0, 0)

    @pl.when((i < nu) & ((i == 0) | (te_ref[i] != te_ref[jnp.maximum(i - 1, 0)])))
    def _():
        wg_bf[...] = wg_ref[...].astype(BF16)
        wu_bf[...] = wu_ref[...].astype(BF16)
        wd_bf[...] = wd_ref[...].astype(BF16)

    def step(slot):
        nxt = jnp.minimum(i + 1, nu - 1)
        issue(nxt, 1 - slot, range(0, TMM // 4))
        drain(slot)
        issue(nxt, 1 - slot, range(TMM // 4, TMM))
        x = xbuf[slot][...].astype(BF16)
        hid = jnp.dot(x, wg_bf[...], preferred_element_type=F32)
        hid = hid * jax.nn.sigmoid(hid) * jnp.dot(x, wu_bf[...], preferred_element_type=F32)
        y_ref[...] = jnp.dot(hid.astype(BF16), wd_bf[...], preferred_element_type=F32)

        @pl.when(i == nu - 1)
        def _():
            drain(1 - slot)

    for slot in range(2):
        @pl.when((i % 2 == slot) & (i < nu))
        def _():
            step(slot)

    @pl.when(i >= nu)
    def _():
        y_ref[...] = jnp.zeros_like(y_ref)


def _ffn(h2, inv, tile_expert, n_used, wg, wu, wd, layer):
    ns = inv.shape[0]
    wspec = lambda *shape: pl.BlockSpec((None, None) + shape, lambda i, te, nu, iv: (layer, te[i], 0, 0))
    return pl.pallas_call(
        _ffn_kernel,
        grid_spec=pltpu.PrefetchScalarGridSpec(
            num_scalar_prefetch=3,
            grid=(ns // TMM,),
            in_specs=[
                pl.BlockSpec(memory_space=pl.ANY),
                wspec(D_MODEL, D_EXPERT), wspec(D_MODEL, D_EXPERT), wspec(D_EXPERT, D_MODEL),
            ],
            out_specs=pl.BlockSpec((TMM, D_MODEL), lambda i, te, nu, iv: (i, 0)),
            scratch_shapes=[pltpu.VMEM((TMM, D_MODEL), F32), pltpu.VMEM((TMM, D_MODEL), F32),
                            pltpu.SemaphoreType.DMA((2,)),
                            pltpu.VMEM((D_MODEL, D_EXPERT), BF16), pltpu.VMEM((D_MODEL, D_EXPERT), BF16),
                            pltpu.VMEM((D_EXPERT, D_MODEL), BF16)],
        ),
        out_shape=jax.ShapeDtypeStruct((ns, D_MODEL), F32),
        compiler_params=_cparams(("arbitrary",)),
        name="moe_ffn",
    )(tile_expert, n_used, inv, h2, wg, wu, wd)


def _combine_kernel(s1_ref, s2_ref, x_ref, rt_ref, mod_ref, g_ref, ys_hbm, o_ref, buf, sem, *,
                    first_tile, final):
    i = pl.program_id(0)
    nt = pl.num_programs(0)

    def issue(tile, slot):
        for r in range(TM):
            t = (first_tile + tile) * TM + r
            _row_copy(ys_hbm, s1_ref[t], buf.at[slot], r, sem.at[slot]).start()
            _row_copy(ys_hbm, s2_ref[t], buf.at[slot], TM + r, sem.at[slot]).start()

    def drain(slot):
        for r in range(2 * TM):
            _row_copy(ys_hbm, 0, buf.at[slot], 0, sem.at[slot]).wait()

    @pl.when(i == 0)
    def _():
        issue(0, 0)

    def step(slot):
        issue(jnp.minimum(i + 1, nt - 1), 1 - slot)
        drain(slot)
        rt = rt_ref[...]
        y = rt[:, 2:3] * buf[slot, 0:TM, :] + rt[:, 3:4] * buf[slot, TM:2 * TM, :]
        out = x_ref[...] + mod_ref[5:6, :] * y
        if final:
            out = out * lax.rsqrt(jnp.mean(out * out, axis=-1, keepdims=True) + NORM_EPS) * g_ref[...]
        o_ref[...] = out

        @pl.when(i == nt - 1)
        def _():
            drain(1 - slot)

    for slot in range(2):
        @pl.when(i % 2 == slot)
        def _():
            step(slot)


def _combine(xn, rt, mod, g, ys, s1, s2, n_ctx_tiles, final):
    n = xn.shape[0]
    first = n_ctx_tiles if final else 0
    tiles = n // TM - first
    kind = lambda i, a, b: (jnp.where(i + first < n_ctx_tiles, 0, 1), 0, 0)
    return pl.pallas_call(
        functools.partial(_combine_kernel, first_tile=first, final=final),
        grid_spec=pltpu.PrefetchScalarGridSpec(
            num_scalar_prefetch=2,
            grid=(tiles,),
            in_specs=[
                pl.BlockSpec((TM, D_MODEL), lambda i, a, b: (i + first, 0)),
                pl.BlockSpec((TM, SUBLANES), lambda i, a, b: (i + first, 0)),
                pl.BlockSpec((None, N_MOD, D_MODEL), kind),
                pl.BlockSpec((1, D_MODEL), lambda i, a, b: (0, 0)),
                pl.BlockSpec(memory_space=pl.ANY),
            ],
            out_specs=pl.BlockSpec((TM, D_MODEL), lambda i, a, b: (i, 0)),
            scratch_shapes=[
                pltpu.VMEM((2, 2 * TM, D_MODEL), F32),
                pltpu.SemaphoreType.DMA((2,)),
            ],
        ),
        out_shape=jax.ShapeDtypeStruct((tiles * TM, D_MODEL), F32),
        compiler_params=_cparams(("arbitrary",)),
        name="moe_combine",
    )(s1, s2, xn, rt, mod, g, ys)


def _pre_gather_kernel(s1_ref, s2_ref, x_ref, rt_ref, modp_ref, ys_hbm, mod_ref, g_ref, w_ref, pm_ref,
                       xc_ref, u_ref, z_ref, gate_ref, buf, sem, wbf_ref):
    i = pl.program_id(0)
    nt = pl.num_programs(0)

    @pl.when(i == 0)
    def _():
        wbf_ref[...] = w_ref[...].astype(BF16)

    def issue(tile, slot, rows=range(TM)):
        for r in rows:
            t = tile * TM + r
            _row_copy(ys_hbm, s1_ref[t], buf.at[slot], r, sem.at[slot]).start()
            _row_copy(ys_hbm, s2_ref[t], buf.at[slot], TM + r, sem.at[slot]).start()

    def drain(slot):
        for r in range(2 * TM):
            _row_copy(ys_hbm, 0, buf.at[slot], 0, sem.at[slot]).wait()

    @pl.when(i == 0)
    def _():
        issue(0, 0)

    def step(slot):
        nxt = jnp.minimum(i + 1, nt - 1)
        issue(nxt, 1 - slot, range(0, TM // 8))
        drain(slot)
        issue(nxt, 1 - slot, range(TM // 8, TM))
        rt = rt_ref[...]
        y = rt[:, 2:3] * buf[slot, 0:TM, :] + rt[:, 3:4] * buf[slot, TM:2 * TM, :]
        x = x_ref[...] + modp_ref[5:6, :] * y
        xc_ref[...] = x
        _pre_project(x, mod_ref, g_ref, wbf_ref, pm_ref, u_ref, z_ref, gate_ref)

        @pl.when(i == nt - 1)
        def _():
            drain(1 - slot)

    for slot in range(2):
        @pl.when(i % 2 == slot)
        def _():
            step(slot)


def _pre_gather(pending, mod, g, w_in, layer, pm, n_ctx_tiles):
    xn, rt, mod_prev, ys, s1, s2 = pending
    n = xn.shape[0]
    kind = lambda i, a, b: (jnp.where(i < n_ctx_tiles, 0, 1), 0, 0)
    tile = lambda w: pl.BlockSpec((TM, w), lambda i, a, b: (i, 0))
    return pl.pallas_call(
        _pre_gather_kernel,
        grid_spec=pltpu.PrefetchScalarGridSpec(
            num_scalar_prefetch=2,
            grid=(n // TM,),
            in_specs=[
                tile(D_MODEL), tile(SUBLANES),
                pl.BlockSpec((None, N_MOD, D_MODEL), kind),
                pl.BlockSpec(memory_space=pl.ANY),
                pl.BlockSpec((None, N_MOD, D_MODEL), kind),
                pl.BlockSpec((1, D_MODEL), lambda i, a, b: (0, 0)),
                pl.BlockSpec((None, D_MODEL, IN_COLS), lambda i, a, b: (layer, 0, 0),
                             pipeline_mode=pl.Buffered(1)),
                pl.BlockSpec((TM, TM), lambda i, a, b: (0, 0)),
            ],
            out_specs=[tile(D_MODEL), tile(SSM_WIDTH), tile(CONV_WIDTH), tile(2 * D_MODEL)],
            scratch_shapes=[
                pltpu.VMEM((2, 2 * TM, D_MODEL), F32),
                pltpu.SemaphoreType.DMA((2,)),
                pltpu.VMEM((D_MODEL, IN_COLS), BF16),
            ],
        ),
        out_shape=[
            jax.ShapeDtypeStruct((n, D_MODEL), F32),
            jax.ShapeDtypeStruct((n, SSM_WIDTH), BF16),
            jax.ShapeDtypeStruct((n, CONV_WIDTH), F32),
            jax.ShapeDtypeStruct((n, 2 * D_MODEL), BF16),
        ],
        compiler_params=_cparams(("arbitrary",)),
        name="pre_gather",
    )(s1, s2, xn, rt, mod_prev, ys, mod, g, w_in, pm)


def _moe(xn, h2, r, counts, mod, wg, wu, wd, layer, final_g, n_ctx_tiles, final):
    n = xn.shape[0]
    ns = 2 * n + N_EXPERTS * TMM
    cnt = counts[:, 0].astype(jnp.int32)
    padded = (cnt + TMM - 1) // TMM * TMM
    ends = jnp.cumsum(padded)
    off = ends - padded
    experts = jnp.arange(N_EXPERTS, dtype=jnp.int32)

    def slot(e_row, rank_row):
        hit = e_row.astype(jnp.int32)[:, None] == experts[None, :]
        return jnp.sum(jnp.where(hit, off[None, :], 0), axis=1) + rank_row.astype(jnp.int32)

    s1 = slot(r[0], r[4])
    s2 = slot(r[1], r[5])
    tile_start = jnp.arange(ns // TMM, dtype=jnp.int32) * TMM
    tile_expert = jnp.minimum(jnp.sum(ends[None, :] <= tile_start[:, None], axis=1),
                              N_EXPERTS - 1).astype(jnp.int32)
    n_used = (ends[-1:] // TMM).astype(jnp.int32)
    inv = _invert(s1, s2, off + cnt, ends, ns)
    ys = _ffn(h2, inv, tile_expert, n_used, wg, wu, wd, layer)
    if final:
        return _combine(xn, r.T, mod, final_g, ys, s1, s2, n_ctx_tiles, final)
    return xn, r.T, mod, ys, s1, s2


def kernel(x, c, ctx, c_ctx, ada_w, ada_b, norm_mix_g, norm_ffn_g, w_in, ssm_a_re, ssm_a_im, ssm_log_dt, ssm_b_re, ssm_b_im, ssm_c_re, ssm_c_im, ssm_d, w_a_val, w_a_gate, conv_w, conv_b, conv_ln_g, conv_ln_b, w_b_out, w_out, router_w, router_b, w_exp_gate, w_exp_up, w_exp_down, final_norm_g):
    depth = ada_w.shape[0]
    bsz, seq, _ = x.shape
    n_ctx = ctx.shape[1]
    assert bsz == 1 and n_ctx == TM and seq % TM == 0 and seq % GRID_W == 0
    n_ctx_tiles = n_ctx // TM
    n = n_ctx + seq

    rows = _row_sources(ctx[0], x[0], 0, n_ctx_tiles)
    cc = jnp.zeros((SUBLANES, D_MODEL), F32).at[0].set(c_ctx).at[1].set(c[0])
    mods = _mods(cc, ada_w, ada_b)[:, :2].reshape(depth, 2, N_MOD, D_MODEL)

    rwt = router_w.T
    rwh, rwl = _split_bf16(rwt)
    rb = router_b.reshape(N_EXPERTS, 1)
    pm, pmt = _ssm_perm()

    pending = None
    for l in range(depth):
        last = l == depth - 1
        g_mix = norm_mix_g[l].reshape(1, -1)
        if pending is None:
            u, z, gates = _pre(rows, n, mods[l], g_mix, w_in, l, pm, n_ctx_tiles)
        else:
            xc, u, z, gates = _pre_gather(pending, mods[l], g_mix, w_in, l, pm, n_ctx_tiles)
            rows = _row_sources(xc, xc, n_ctx_tiles, n_ctx_tiles)
        prm =[_ssm_params(ssm_a_re[l, k], ssm_a_im[l, k], ssm_log_dt[l, k], ssm_b_re[l, k],
                           ssm_b_im[l, k], ssm_c_re[l, k], ssm_c_im[l, k]) for k in range(2)]
        ys = _ssm(u, *[jnp.stack(pair) for pair in zip(*prm)], n_ctx_tiles)
        vec = lambda a: a.reshape(1, -1)
        conv_args = (conv_w[l], vec(conv_b[l]), vec(conv_ln_g[l]), vec(conv_ln_b[l]))
        if l % 2 == 0:
            bp = _conv_rows(z, *conv_args, n_ctx_tiles)
        else:
            assert last, "column conv over the context prefix is not needed by the last layer only"
            bp = _conv_cols(z, *conv_args, n_ctx)
        xn, h2, r, counts = _merge(rows, n, u, ys[0], ys[1], bp, gates, mods[l], vec(ssm_d[l]), pmt,
                                   w_a_val, w_a_gate, w_b_out, w_out, l, vec(norm_ffn_g[l]),
                                   rwh, rwl, rb, n_ctx_tiles)
        pending = _moe(xn, h2, r, counts, mods[l], w_exp_gate, w_exp_up, w_exp_down, l,
                       vec(final_norm_g), n_ctx_tiles, final=last)
    return pending[None]
```
